```python
import jax, jax.numpy as jnp
from jax import lax
import numpy as np

D_MODEL = 1024
BATCH = 8
SEQ = 2048
DEPTH = 4
DEC_BATCH = 128
DEC_SEQ = 1
PAST_LEN = 2048
PAGE_SIZE = 128

N_MIXERS = 3
EPS = 1e-6
SB_HEADS = 16
SB_HEAD_DIM = D_MODEL // SB_HEADS
Q_BLOCK = 128
CHUNK = 128
CM_HALF = 3 * D_MODEL
CM_GROUPS = 8
CM_GROUP_DIM = CM_HALF // CM_GROUPS
ML_INNER = 2 * D_MODEL
ML_HEADS = 4
ML_HEAD_DIM = ML_INNER // ML_HEADS
ML_CONV = 4
ML_CHUNK = 128
D_FF = 2816
FFN_CONV = 3
PLE_DIM = 256
N_SB = (DEPTH + 2) // 3
N_CM = (DEPTH + 1) // 3
N_ML = DEPTH // 3

kernel_name = 'hybrid_sb_gmlp_mlstm_decoder_step'


def rmsnorm(x, g):
    xf = x.astype(jnp.float32)
    y = xf * lax.rsqrt(jnp.mean(xf * xf, axis=-1, keepdims=True) + EPS)
    return (y * g.astype(jnp.float32)).astype(x.dtype)


def layernorm(x, g, b):
    xf = x.astype(jnp.float32)
    mu = jnp.mean(xf, axis=-1, keepdims=True)
    var = jnp.mean(jnp.square(xf - mu), axis=-1, keepdims=True)
    y = (xf - mu) * lax.rsqrt(var + EPS) * g.astype(jnp.float32) + b.astype(jnp.float32)
    return y.astype(x.dtype)


def causal_dwconv(x, buf, w, b):
    T = x.shape[1]
    width = w.shape[0]
    xpad = jnp.concatenate([buf.astype(x.dtype), x], axis=1)
    y = b
    for j in range(width):
        y = y + w[j] * xpad[:, j:j + T]
    return y, xpad[:, T:]


def sb_block(q, q_pos, k, v, k_pos):
    z = jnp.einsum('bqhd,bkhd->bhqk', q, k).astype(jnp.float32) * (SB_HEAD_DIM ** -0.5)
    mask = k_pos[None, :] < q_pos[:, None]
    c = jnp.where(mask, jax.nn.log_sigmoid(-z), 0.0)
    cum = jnp.cumsum(c, axis=-1)
    log_a = jax.nn.log_sigmoid(z) + cum[..., -1:] - cum
    a = jnp.where(mask, jnp.exp(log_a), 0.0)
    return jnp.einsum('bhqk,bkhd->bqhd', a.astype(v.dtype), v)


def sb_prompt(q, k, v):
    B, T = q.shape[:2]
    qb = min(Q_BLOCK, T)
    nb = T // qb
    k_pos = jnp.arange(T)

    def one_block(bi):
        start = bi * qb
        q_blk = lax.dynamic_slice_in_dim(q, start, qb, axis=1)
        return sb_block(q_blk, start + jnp.arange(qb), k, v, k_pos)

    o = lax.map(one_block, jnp.arange(nb))
    return jnp.moveaxis(o, 0, 1).reshape(B, T, SB_HEADS, SB_HEAD_DIM)


def chunk_mlp(h, w_in, b_in, ln_g, ln_b, w_s, b_s, w_o):
    B, T, _ = h.shape
    L = min(CHUNK, T)
    nc = T // L
    a = jax.nn.gelu(h @ w_in + b_in)
    u, v = a[..., :CM_HALF], a[..., CM_HALF:]
    v = layernorm(v, ln_g, ln_b)
    ws = w_s[:, :L, :L] * jnp.tril(jnp.ones((L, L), w_s.dtype))
    s = jnp.einsum('gts,bcsgd->bctgd', ws, v.reshape(B, nc, L, CM_GROUPS, CM_GROUP_DIM))
    s = s + b_s[:, :L].T[:, :, None]
    y = u * s.reshape(B, T, CM_HALF)
    return y @ w_o, v


def mlstm_chunkwise(q, k, v, i_gate, log_f, C0, n0, m0):
    B, T, H, DK = q.shape
    DV = v.shape[-1]
    L = min(ML_CHUNK, T)
    nc = T // L
    f32 = jnp.float32

    def chunks(a):
        a = a.astype(f32).reshape((B, nc, L) + a.shape[2:])
        return jnp.swapaxes(jnp.swapaxes(a, 0, 1), 2, 3)

    causal = jnp.tril(jnp.ones((L, L), bool))

    def step(carry, xs):
        C, n, m = carry
        qc, kc, vc, ic, fc = xs
        b = jnp.cumsum(fc, axis=-1)
        d_log = jnp.where(causal, b[..., :, None] - b[..., None, :] + ic[..., None, :], -jnp.inf)
        inter = b + m[..., None]
        m_t = jnp.maximum(inter, jnp.max(d_log, axis=-1))
        w_intra = jnp.exp(d_log - m_t[..., None])
        w_inter = jnp.exp(inter - m_t)
        qk = jnp.einsum('bhtd,bhsd->bhts', qc, kc) * w_intra
        num = w_inter[..., None] * jnp.einsum('bhtd,bhde->bhte', qc, C) + jnp.einsum('bhts,bhse->bhte', qk, vc)
        den = w_inter * jnp.einsum('bhtd,bhd->bht', qc, n) + jnp.sum(qk, axis=-1)
        h = num / jnp.maximum(jnp.abs(den), jnp.exp(-m_t))[..., None]
        m_new = m_t[..., -1]
        w_state = jnp.exp(b[..., -1] + m - m_new)
        w_rows = jnp.exp(b[..., -1:] - b + ic - m_new[..., None])
        C_new = w_state[..., None, None] * C + jnp.einsum('bhsd,bhse->bhde', kc * w_rows[..., None], vc)
        n_new = w_state[..., None] * n + jnp.einsum('bhs,bhsd->bhd', w_rows, kc)
        return (C_new, n_new, m_new), h

    (C, n, m), h = lax.scan(step, (C0.astype(f32), n0.astype(f32), m0.astype(f32)),
                            (chunks(q), chunks(k), chunks(v), chunks(i_gate), chunks(log_f)))
    h = jnp.swapaxes(jnp.swapaxes(h, 2, 3), 0, 1).reshape(B, T, H, DV)
    return h, C, n, m


def mlstm_mixer(h, w_in, b_gates, conv_w, conv_b, w_q, w_k, w_v, norm_g, w_o, C0, n0, m0, conv_buf):
    B, T, _ = h.shape
    proj = h @ w_in
    xm = proj[..., :ML_INNER]
    o_pre = proj[..., ML_INNER:2 * ML_INNER]
    gates = (proj[..., 2 * ML_INNER:] + b_gates).astype(jnp.float32)
    i_pre, f_pre = gates[..., :ML_HEADS], gates[..., ML_HEADS:]
    xc, new_conv = causal_dwconv(xm, conv_buf, conv_w, conv_b)
    xc = jax.nn.silu(xc)
    xc_h = xc.reshape(B, T, ML_HEADS, ML_HEAD_DIM)
    xm_h = xm.reshape(B, T, ML_HEADS, ML_HEAD_DIM)
    q = jnp.einsum('bthd,hde->bthe', xc_h, w_q)
    k = jnp.einsum('bthd,hde->bthe', xc_h, w_k) * (ML_HEAD_DIM ** -0.5)
    v = jnp.einsum('bthd,hde->bthe', xm_h, w_v)
    hh, C, n, m = mlstm_chunkwise(q, k, v, i_pre, jax.nn.log_sigmoid(f_pre), C0, n0, m0)
    mu = jnp.mean(hh, axis=-1, keepdims=True)
    var = jnp.mean(jnp.square(hh - mu), axis=-1, keepdims=True)
    hn = ((hh - mu) * lax.rsqrt(var + EPS)).reshape(B, T, ML_INNER) * norm_g.astype(jnp.float32)
    out = jax.nn.sigmoid(o_pre) * hn.astype(h.dtype)
    return out @ w_o, C, n, m, new_conv


def conv_ffn(h, w_up, conv_w, conv_b, w_down, buf):
    a = h @ w_up
    c, new_buf = causal_dwconv(a, buf, conv_w, conv_b)
    gate, up = c[..., :D_FF], c[..., D_FF:]
    return (jax.nn.silu(gate) * up) @ w_down, new_buf


def _trunk(x, p, past, ml_state, ffn_buf0, weights):
    (norm_mix, norm_ffn, norm_ple, norm_final, sb_w_qkv, sb_w_o, cm_w_in, cm_b_in, cm_ln_g, cm_ln_b,
     cm_w_s, cm_b_s, cm_w_o, ml_w_in, ml_b_gates, ml_conv_w, ml_conv_b, ml_w_q, ml_w_k, ml_w_v,
     ml_norm_g, ml_w_o, ffn_w_up, ffn_conv_w, ffn_conv_b, ffn_w_down, ple_w_proj, ple_w_gate) = weights
    B, T, _ = x.shape
    ml_C0, ml_n0, ml_m0, ml_conv0 = ml_state
    k_rows, v_rows, cm_rows, ml_C, ml_n, ml_m, ml_conv, ffn_bufs = [], [], [], [], [], [], [], []
    r = x
    for i in range(DEPTH):
        kind, j = i % N_MIXERS, i // N_MIXERS
        h = rmsnorm(r, norm_mix[i])
        if kind == 0:
            qkv = (h @ sb_w_qkv[j]).reshape(B, T, 3, SB_HEADS, SB_HEAD_DIM)
            q, k, v = qkv[:, :, 0], qkv[:, :, 1], qkv[:, :, 2]
            if past is None:
                o = sb_prompt(q, k, v)
            else:
                page_table, cache_k, cache_v = past
                n_past = page_table.shape[1] * PAGE_SIZE
                k_past = cache_k[j][page_table].reshape(B, n_past, SB_HEADS, SB_HEAD_DIM).astype(k.dtype)
                v_past = cache_v[j][page_table].reshape(B, n_past, SB_HEADS, SB_HEAD_DIM).astype(v.dtype)
                k_all = jnp.concatenate([k_past, k], axis=1)
                v_all = jnp.concatenate([v_past, v], axis=1)
                o = sb_block(q, n_past + jnp.arange(T), k_all, v_all, jnp.arange(n_past + T))
            mix = o.reshape(B, T, D_MODEL) @ sb_w_o[j]
            k_rows.append(k)
            v_rows.append(v)
        elif kind == 1:
            mix, v_new = chunk_mlp(h, cm_w_in[j], cm_b_in[j], cm_ln_g[j], cm_ln_b[j], cm_w_s[j], cm_b_s[j], cm_w_o[j])
            cm_rows.append(v_new)
        else:
            mix, C, n, m, cb = mlstm_mixer(h, ml_w_in[j], ml_b_gates[j], ml_conv_w[j], ml_conv_b[j], ml_w_q[j],
                                           ml_w_k[j], ml_w_v[j], ml_norm_g[j], ml_w_o[j],
                                           ml_C0[j], ml_n0[j], ml_m0[j], ml_conv0[j])
            ml_C.append(C)
            ml_n.append(n)
            ml_m.append(m)
            ml_conv.append(cb)
        r = r + mix
        f, buf = conv_ffn(rmsnorm(r, norm_ffn[i]), ffn_w_up[i], ffn_conv_w[i], ffn_conv_b[i], ffn_w_down[i], ffn_buf0[i])
        r = r + f
        ffn_bufs.append(buf)
        gate = jax.nn.sigmoid(rmsnorm(r, norm_ple[i]) @ ple_w_gate[i])
        r = r + gate * (p[i] @ ple_w_proj[i])
    y = rmsnorm(r, norm_final)
    return (y, jnp.stack(k_rows), jnp.stack(v_rows), jnp.stack(cm_rows), jnp.stack(ml_C), jnp.stack(ml_n),
            jnp.stack(ml_m), jnp.stack(ml_conv), jnp.stack(ffn_bufs))


def setup_inputs(seed: int = 0) -> dict:
    key = jax.random.key(seed)
    keys = iter(jax.random.split(key, 64))

    def normal(shape, scale):
        return scale * jax.random.normal(next(keys), shape, jnp.float32)

    def gain(shape):
        return 1.0 + normal(shape, 0.01)

    n_pages = PAST_LEN // PAGE_SIZE
    n_phys = (DEC_BATCH * n_pages * 5) // 4
    page_table = jax.random.permutation(next(keys), n_phys)[:DEC_BATCH * n_pages]
    page_table = page_table.reshape(DEC_BATCH, n_pages).astype(jnp.int32)
    f_bias = jax.random.uniform(next(keys), (N_ML, ML_HEADS), jnp.float32, 3.0, 6.0)
    ml_b_gates = jnp.concatenate([normal((N_ML, ML_HEADS), 0.1), f_bias], axis=-1)
    m_state = jax.random.uniform(next(keys), (N_ML, DEC_BATCH, ML_HEADS), jnp.float32, 0.0, 3.0)
    return {
        'x_prompt': normal((BATCH, SEQ, D_MODEL), 1.0),
        'x_sample': normal((DEC_BATCH, DEC_SEQ, D_MODEL), 1.0),
        'p_prompt': normal((DEPTH, BATCH, SEQ, PLE_DIM), 1.0),
        'p_sample': normal((DEPTH, DEC_BATCH, DEC_SEQ, PLE_DIM), 1.0),
        'page_table': page_table,
        'cache_k': normal((N_SB, n_phys, PAGE_SIZE, SB_HEADS, SB_HEAD_DIM), 1.0),
        'cache_v': normal((N_SB, n_phys, PAGE_SIZE, SB_HEADS, SB_HEAD_DIM), 1.0),
        'state_mlstm_c': normal((N_ML, DEC_BATCH, ML_HEADS, ML_HEAD_DIM, ML_HEAD_DIM), 0.05),
        'state_mlstm_n': normal((N_ML, DEC_BATCH, ML_HEADS, ML_HEAD_DIM), 0.05),
        'state_mlstm_m': m_state,
        'state_mlstm_conv': normal((N_ML, DEC_BATCH, ML_CONV - 1, ML_INNER), 1.0),
        'state_ffn_conv': normal((DEPTH, DEC_BATCH, FFN_CONV - 1, 2 * D_FF), 1.0),
        'norm_mix': gain((DEPTH, D_MODEL)),
        'norm_ffn': gain((DEPTH, D_MODEL)),
        'norm_ple': gain((DEPTH, D_MODEL)),
        'norm_final': gain((D_MODEL,)),
        'sb_w_qkv': normal((N_SB, D_MODEL, 3 * D_MODEL), D_MODEL ** -0.5),
        'sb_w_o': normal((N_SB, D_MODEL, D_MODEL), D_MODEL ** -0.5),
        'cm_w_in': normal((N_CM, D_MODEL, 2 * CM_HALF), D_MODEL ** -0.5),
        'cm_b_in': normal((N_CM, 2 * CM_HALF), 0.01),
        'cm_ln_g': gain((N_CM, CM_HALF)),
        'cm_ln_b': normal((N_CM, CM_HALF), 0.01),
        'cm_w_s': normal((N_CM, CM_GROUPS, CHUNK, CHUNK), CHUNK ** -0.5),
        'cm_b_s': gain((N_CM, CM_GROUPS, CHUNK)),
        'cm_w_o': normal((N_CM, CM_HALF, D_MODEL), CM_HALF ** -0.5),
        'ml_w_in': normal((N_ML, D_MODEL, 2 * ML_INNER + 2 * ML_HEADS), D_MODEL ** -0.5),
        'ml_b_gates': ml_b_gates,
        'ml_conv_w': normal((N_ML, ML_CONV, ML_INNER), ML_CONV ** -0.5),
        'ml_conv_b': normal((N_ML, ML_INNER), 0.01),
        'ml_w_q': normal((N_ML, ML_HEADS, ML_HEAD_DIM, ML_HEAD_DIM), ML_HEAD_DIM ** -0.5),
        'ml_w_k': normal((N_ML, ML_HEADS, ML_HEAD_DIM, ML_HEAD_DIM), ML_HEAD_DIM ** -0.5),
        'ml_w_v': normal((N_ML, ML_HEADS, ML_HEAD_DIM, ML_HEAD_DIM), ML_HEAD_DIM ** -0.5),
        'ml_norm_g': gain((N_ML, ML_INNER)),
        'ml_w_o': normal((N_ML, ML_INNER, D_MODEL), ML_INNER ** -0.5),
        'ffn_w_up': normal((DEPTH, D_MODEL, 2 * D_FF), D_MODEL ** -0.5),
        'ffn_conv_w': normal((DEPTH, FFN_CONV, 2 * D_FF), FFN_CONV ** -0.5),
        'ffn_conv_b': normal((DEPTH, 2 * D_FF), 0.01),
        'ffn_w_down': normal((DEPTH, D_FF, D_MODEL), D_FF ** -0.5),
        'ple_w_proj': normal((DEPTH, PLE_DIM, D_MODEL), PLE_DIM ** -0.5),
        'ple_w_gate': normal((DEPTH, D_MODEL, D_MODEL), D_MODEL ** -0.5),
    }


def reference(x_prompt, x_sample, p_prompt, p_sample, page_table, cache_k, cache_v, state_mlstm_c,
              state_mlstm_n, state_mlstm_m, state_mlstm_conv, state_ffn_conv, norm_mix, norm_ffn, norm_ple,
              norm_final, sb_w_qkv, sb_w_o, cm_w_in, cm_b_in, cm_ln_g, cm_ln_b, cm_w_s, cm_b_s, cm_w_o,
              ml_w_in, ml_b_gates, ml_conv_w, ml_conv_b, ml_w_q, ml_w_k, ml_w_v, ml_norm_g, ml_w_o,
              ffn_w_up, ffn_conv_w, ffn_conv_b, ffn_w_down, ple_w_proj, ple_w_gate):
    weights = (norm_mix, norm_ffn, norm_ple, norm_final, sb_w_qkv, sb_w_o, cm_w_in, cm_b_in, cm_ln_g, cm_ln_b,
               cm_w_s, cm_b_s, cm_w_o, ml_w_in, ml_b_gates, ml_conv_w, ml_conv_b, ml_w_q, ml_w_k, ml_w_v,
               ml_norm_g, ml_w_o, ffn_w_up, ffn_conv_w, ffn_conv_b, ffn_w_down, ple_w_proj, ple_w_gate)
    B = x_prompt.shape[0]
    ml_zero = (jnp.zeros((N_ML, B, ML_HEADS, ML_HEAD_DIM, ML_HEAD_DIM), jnp.float32),
               jnp.zeros((N_ML, B, ML_HEADS, ML_HEAD_DIM), jnp.float32),
               jnp.zeros((N_ML, B, ML_HEADS), jnp.float32),
               jnp.zeros((N_ML, B, ML_CONV - 1, ML_INNER), x_prompt.dtype))
    ffn_zero = jnp.zeros((DEPTH, B, FFN_CONV - 1, 2 * D_FF), x_prompt.dtype)
    (y_prompt, k_prompt, v_prompt, _cm_v_prompt, mlstm_c_prompt, mlstm_n_prompt, mlstm_m_prompt,
     mlstm_conv_prompt, ffn_conv_prompt) = _trunk(x_prompt, p_prompt, None, ml_zero, ffn_zero, weights)
    (y_sample, k_sample, v_sample, cm_v_sample, mlstm_c_sample, mlstm_n_sample, mlstm_m_sample,
     mlstm_conv_sample, ffn_conv_sample) = _trunk(
        x_sample, p_sample, (page_table, cache_k, cache_v),
        (state_mlstm_c, state_mlstm_n, state_mlstm_m, state_mlstm_conv), state_ffn_conv, weights)
    return (y_prompt, y_sample, k_prompt, v_prompt, k_sample, v_sample, cm_v_sample,
            mlstm_c_prompt, mlstm_n_prompt, mlstm_m_prompt, mlstm_conv_prompt,
            mlstm_c_sample, mlstm_n_sample, mlstm_m_sample, mlstm_conv_sample,
            ffn_conv_prompt, ffn_conv_sample)
```

```python
import functools

import jax
import jax.numpy as jnp
from jax import lax
from jax.experimental import pallas as pl
from jax.experimental.pallas import tpu as pltpu

F32 = jnp.float32
BF16 = jnp.bfloat16
EPS = 1e-6

VMEM_LIMIT_BYTES = 56 * 1024 * 1024
LANES = 128
SUBLANES = 8

SB_HEAD_DIM = 64
SB_BLOCK = 256
SB_SEG = SB_BLOCK // SUBLANES
PAGE = 128
CM_GROUPS = 8
CM_CHUNK = 128
ML_HEADS = 4
ML_CHUNK = 128
ML_CONV = 4
FFN_CONV = 3
FFN_HALO = 16
ML_HALO = 8


def _cparams(*sem):
    return pltpu.CompilerParams(dimension_semantics=sem, vmem_limit_bytes=VMEM_LIMIT_BYTES)


def _dot(a, b):
    return jnp.dot(a, b, preferred_element_type=F32)


def _dot_nt(a, b):
    return lax.dot_general(a, b, (((1,), (1,)), ((), ())), preferred_element_type=F32)


def _dot_tn(a, b):
    return lax.dot_general(a, b, (((0,), (0,)), ((), ())), preferred_element_type=F32)


def _rms(x, g):
    ms = jnp.mean(x * x, axis=-1, keepdims=True)
    return x * lax.rsqrt(ms + EPS) * g


def _sigmoid(x):
    return 1.0 / (1.0 + jnp.exp(-x))


def _silu(x):
    return x * _sigmoid(x)


def _gelu(x):
    return 0.5 * x * (1.0 + jnp.tanh(0.7978845608028654 * (x + 0.044715 * (x * x * x))))


def _softplus(x):
    return jnp.maximum(x, 0.0) + jnp.log(1.0 + jnp.exp(-jnp.abs(x)))


def _log_sigmoid(x):
    return jnp.minimum(x, 0.0) - jnp.log(1.0 + jnp.exp(-jnp.abs(x)))


def _split3(x):
    hi = x.astype(BF16)
    r1 = x - hi.astype(F32)
    mid = r1.astype(BF16)
    lo = (r1 - mid.astype(F32)).astype(BF16)
    return hi, mid, lo


def _dot_exact_lhs(lhs01, x):
    hi, mid, lo = _split3(x)
    return _dot(lhs01, hi) + _dot(lhs01, mid) + _dot(lhs01, lo)


def _dot_exact_rhs(x, rhs01):
    hi, mid, lo = _split3(x)
    return _dot(hi, rhs01) + _dot(mid, rhs01) + _dot(lo, rhs01)


def _pick_tile(n, candidates):
    for c in candidates:
        if n % c == 0:
            return c
    return n


def _mm_kernel(*refs, norm, act, has_bias, has_res):
    it = iter(refs)
    x_ref = next(it)
    g_ref = next(it) if norm else None
    w_ref = next(it)
    b_ref = next(it) if has_bias else None
    r_ref = next(it) if has_res else None
    o_ref = next(it)
    xn_ref = next(it)

    @pl.when(pl.program_id(1) == 0)
    def _():
        x = x_ref[...].astype(F32)
        if norm:
            x = _rms(x, g_ref[...])
        xn_ref[...] = x.astype(BF16)

    acc = _dot(xn_ref[...], w_ref[...])
    if has_bias:
        acc = acc + b_ref[...]
    if act == "gelu":
        acc = _gelu(acc)
    if has_res:
        acc = acc + r_ref[...]
    o_ref[...] = acc.astype(o_ref.dtype)


def _matmul(x, w, *, g=None, bias=None, res=None, act=None, out_dtype=F32, name="matmul"):
    m, k = x.shape
    n = w.shape[1]
    tm = _pick_tile(m, (512, 256, 128, 16, 8))
    tn = _pick_tile(n, (512, 384, 256, 128))
    norm = g is not None
    args = [x]
    specs = [pl.BlockSpec((tm, k), lambda i, j: (i, 0))]
    if norm:
        args.append(g.reshape(1, k))
        specs.append(pl.BlockSpec((1, k), lambda i, j: (0, 0)))
    args.append(w)
    specs.append(pl.BlockSpec((k, tn), lambda i, j: (0, j)))
    if bias is not None:
        args.append(bias.reshape(1, n))
        specs.append(pl.BlockSpec((1, tn), lambda i, j: (0, j)))
    if res is not None:
        args.append(res)
        specs.append(pl.BlockSpec((tm, tn), lambda i, j: (i, j)))
    kern = functools.partial(_mm_kernel, norm=norm, act=act, has_bias=bias is not None,
                             has_res=res is not None)
    return pl.pallas_call(
        kern,
        grid=(m // tm, n // tn),
        in_specs=specs,
        out_specs=pl.BlockSpec((tm, tn), lambda i, j: (i, j)),
        out_shape=jax.ShapeDtypeStruct((m, n), out_dtype),
        scratch_shapes=[pltpu.VMEM((tm, k), BF16)],
        compiler_params=_cparams("parallel", "arbitrary"),
        name=name,
    )(*args)


def _ffn_seq_kernel(x_ref, halo_ref, g_ref, wg_ref, wu_ref, cwg_ref, cwu_ref, cbg_ref, cbu_ref, wd_ref,
                    o_ref, xn_ref, a_ref, acc_ref, *, tiles_per_seq, tm, tf):
    i = pl.program_id(0)
    j = pl.program_id(1)

    @pl.when(j == 0)
    def _():
        g = g_ref[...]
        xn_ref[FFN_HALO:, :] = _rms(x_ref[...], g).astype(BF16)
        hal = _rms(halo_ref[...], g)
        hal = jnp.where(i % tiles_per_seq == 0, 0.0, hal)
        xn_ref[:FFN_HALO, :] = hal.astype(BF16)
        acc_ref[...] = jnp.zeros_like(acc_ref)

    xa = xn_ref[...]
    a_ref[:, :tf] = _dot(xa, wg_ref[...])
    a_ref[:, tf:] = _dot(xa, wu_ref[...])

    def conv(lo, cw_ref, cb_ref):
        y = cb_ref[...]
        for t in range(FFN_CONV):
            off = FFN_HALO - (FFN_CONV - 1) + t
            y = y + cw_ref[t:t + 1, :] * a_ref[off:off + tm, lo:lo + tf]
        return y

    cg = conv(0, cwg_ref, cbg_ref)
    cu = conv(tf, cwu_ref, cbu_ref)
    c = (_silu(cg) * cu).astype(BF16)
    acc_ref[...] += _dot(c, wd_ref[...])

    @pl.when(j == pl.num_programs(1) - 1)
    def _():
        o_ref[...] = x_ref[...] + acc_ref[...]


def _ffn_seq(r, g, w_up, conv_w, conv_b, w_down, seq_len):
    m, d = r.shape
    dff = w_down.shape[0]
    tm = _pick_tile(seq_len, (1024, 512, 256, 128))
    tf = 256
    nj = dff // tf
    hb = tm // FFN_HALO
    kern = functools.partial(_ffn_seq_kernel, tiles_per_seq=seq_len // tm, tm=tm, tf=tf)
    return pl.pallas_call(
        kern,
        grid=(m // tm, nj),
        in_specs=[
            pl.BlockSpec((tm, d), lambda i, j: (i, 0)),
            pl.BlockSpec((FFN_HALO, d), lambda i, j: (jnp.maximum(i * hb - 1, 0), 0)),
            pl.BlockSpec((1, d), lambda i, j: (0, 0)),
            pl.BlockSpec((d, tf), lambda i, j: (0, j)),
            pl.BlockSpec((d, tf), lambda i, j: (0, nj + j)),
            pl.BlockSpec((FFN_CONV, tf), lambda i, j: (0, j)),
            pl.BlockSpec((FFN_CONV, tf), lambda i, j: (0, nj + j)),
            pl.BlockSpec((1, tf), lambda i, j: (0, j)),
            pl.BlockSpec((1, tf), lambda i, j: (0, nj + j)),
            pl.BlockSpec((tf, d), lambda i, j: (j, 0)),
        ],
        out_specs=pl.BlockSpec((tm, d), lambda i, j: (i, 0)),
        out_shape=jax.ShapeDtypeStruct((m, d), F32),
        scratch_shapes=[pltpu.VMEM((tm + FFN_HALO, d), BF16),
                        pltpu.VMEM((tm + FFN_HALO, 2 * tf), F32),
                        pltpu.VMEM((tm, d), F32)],
        compiler_params=_cparams("parallel", "arbitrary"),
        name="ffn_seq",
    )(r, r, g.reshape(1, d), w_up, w_up, conv_w, conv_w, conv_b.reshape(1, -1), conv_b.reshape(1, -1), w_down)


def _ffn_step_kernel(ag_ref, au_ref, b0g_ref, b0u_ref, b1g_ref, b1u_ref, cwg_ref, cwu_ref, cbg_ref, cbu_ref,
                     wd_ref, r_ref, o_ref, acc_ref):
    j = pl.program_id(0)

    @pl.when(j == 0)
    def _():
        acc_ref[...] = jnp.zeros_like(acc_ref)

    def conv(a_ref, b0_ref, b1_ref, cw_ref, cb_ref):
        return (cb_ref[...] + cw_ref[0:1, :] * b0_ref[...] + cw_ref[1:2, :] * b1_ref[...]
                + cw_ref[2:3, :] * a_ref[...])

    cg = conv(ag_ref, b0g_ref, b1g_ref, cwg_ref, cbg_ref)
    cu = conv(au_ref, b0u_ref, b1u_ref, cwu_ref, cbu_ref)
    acc_ref[...] += _dot((_silu(cg) * cu).astype(BF16), wd_ref[...])

    @pl.when(j == pl.num_programs(0) - 1)
    def _():
        o_ref[...] = r_ref[...] + acc_ref[...]


def _ffn_step(r, a, buf0, buf1, conv_w, conv_b, w_down):
    m, d = r.shape
    dff = w_down.shape[0]
    tf = 256
    nj = dff // tf
    lo = lambda j: (0, j)
    hi = lambda j: (0, nj + j)
    cb = conv_b.reshape(1, -1)
    return pl.pallas_call(
        _ffn_step_kernel,
        grid=(nj,),
        in_specs=[
            pl.BlockSpec((m, tf), lo), pl.BlockSpec((m, tf), hi),
            pl.BlockSpec((m, tf), lo), pl.BlockSpec((m, tf), hi),
            pl.BlockSpec((m, tf), lo), pl.BlockSpec((m, tf), hi),
            pl.BlockSpec((FFN_CONV, tf), lo), pl.BlockSpec((FFN_CONV, tf), hi),
            pl.BlockSpec((1, tf), lo), pl.BlockSpec((1, tf), hi),
            pl.BlockSpec((tf, d), lambda j: (j, 0)),
            pl.BlockSpec((m, d), lambda j: (0, 0)),
        ],
        out_specs=pl.BlockSpec((m, d), lambda j: (0, 0)),
        out_shape=jax.ShapeDtypeStruct((m, d), F32),
        scratch_shapes=[pltpu.VMEM((m, d), F32)],
        compiler_params=_cparams("arbitrary"),
        name="ffn_step",
    )(a, a, buf0, buf0, buf1, buf1, conv_w, conv_w, cb, cb, w_down, r)


def _ple_kernel(*refs, final):
    if final:
        r_ref, g_ref, wg_ref, p_ref, wp_ref, gf_ref, o_ref = refs
    else:
        r_ref, g_ref, wg_ref, p_ref, wp_ref, o_ref = refs
    r = r_ref[...]
    xn = _rms(r, g_ref[...]).astype(BF16)
    gate = _sigmoid(_dot(xn, wg_ref[...]))
    pp = _dot(p_ref[...].astype(BF16), wp_ref[...])
    out = r + gate * pp
    if final:
        out = _rms(out, gf_ref[...])
    o_ref[...] = out


def _ple(r, g, w_gate, p, w_proj, g_final=None):
    m, d = r.shape
    pd = p.shape[1]
    tm = _pick_tile(m, (512, 256, 128))
    final = g_final is not None
    args = [r, g.reshape(1, d), w_gate, p, w_proj]
    specs = [
        pl.BlockSpec((tm, d), lambda i: (i, 0)),
        pl.BlockSpec((1, d), lambda i: (0, 0)),
        pl.BlockSpec((d, d), lambda i: (0, 0)),
        pl.BlockSpec((tm, pd), lambda i: (i, 0)),
        pl.BlockSpec((pd, d), lambda i: (0, 0)),
    ]
    if final:
        args.append(g_final.reshape(1, d))
        specs.append(pl.BlockSpec((1, d), lambda i: (0, 0)))
    return pl.pallas_call(
        functools.partial(_ple_kernel, final=final),
        grid=(m // tm,),
        in_specs=specs,
        out_specs=pl.BlockSpec((tm, d), lambda i: (i, 0)),
        out_shape=jax.ShapeDtypeStruct((m, d), F32),
        compiler_params=_cparams("parallel"),
        name="ple",
    )(*args)


def _sublane_suffix_exclusive(x):
    idx = lax.broadcasted_iota(jnp.int32, x.shape, 0)
    y = jnp.where(idx + 1 < SUBLANES, pltpu.roll(x, SUBLANES - 1, 0), 0.0)
    y = y + jnp.where(idx + 1 < SUBLANES, pltpu.roll(y, SUBLANES - 1, 0), 0.0)
    y = y + jnp.where(idx + 2 < SUBLANES, pltpu.roll(y, SUBLANES - 2, 0), 0.0)
    y = y + jnp.where(idx + 4 < SUBLANES, pltpu.roll(y, SUBLANES - 4, 0), 0.0)
    return y


def _sb_seq_kernel(q_ref, k_ref, v_ref, o_ref, z_ref, t_ref, a_ref, *, nq):
    tb = SB_BLOCK
    seg_idx = lax.broadcasted_iota(jnp.int32, (SUBLANES, tb), 0) * SB_SEG
    lane_idx = lax.broadcasted_iota(jnp.int32, (SUBLANES, tb), 1)

    def tile(qt, kj, carry, acc, diag):
        k0 = pl.multiple_of(kj * tb, tb)
        z_ref[...] = _dot(k_ref[pl.ds(k0, tb), :], qt)
        run = jnp.zeros((SUBLANES, tb), F32)
        for r in range(SB_SEG - 1, -1, -1):
            zr = z_ref[r * SUBLANES:(r + 1) * SUBLANES, :]
            sp = _softplus(zr)
            if diag:
                sp = jnp.where(seg_idx + r < lane_idx, sp, 0.0)
            run = run + sp
            t_ref[r * SUBLANES:(r + 1) * SUBLANES, :] = zr - run
        cv = carry + _sublane_suffix_exclusive(run)
        for r in range(0, SB_SEG, 2):
            parts = []
            for rr in (r, r + 1):
                a = jnp.exp(t_ref[rr * SUBLANES:(rr + 1) * SUBLANES, :] - cv)
                if diag:
                    a = jnp.where(seg_idx + rr < lane_idx, a, 0.0)
                parts.append(a)
            a_ref[r * SUBLANES:(r + 2) * SUBLANES, :] = jnp.concatenate(parts, axis=0).astype(BF16)
        acc = acc + _dot(v_ref[:, pl.ds(k0, tb)], a_ref[...])
        total = (cv + run)[0:1, :]
        return jnp.broadcast_to(total, (SUBLANES, tb)), acc

    def q_body(qi, _):
        q0 = pl.multiple_of(qi * tb, tb)
        qt = (q_ref[:, pl.ds(q0, tb)].astype(F32) * (SB_HEAD_DIM ** -0.5)).astype(BF16)
        carry = jnp.zeros((SUBLANES, tb), F32)
        acc = jnp.zeros((SB_HEAD_DIM, tb), F32)
        carry, acc = tile(qt, qi, carry, acc, True)

        def kv_body(s, ca):
            return tile(qt, qi - s, ca[0], ca[1], False)

        carry, acc = lax.fori_loop(1, qi + 1, kv_body, (carry, acc))
        o_ref[:, pl.ds(q0, tb)] = acc.astype(o_ref.dtype)
        return 0

    lax.fori_loop(0, nq, q_body, 0)


def _sb_seq(q_t, k_p, v_tp):
    b, h, dh, t = q_t.shape
    nq = t // SB_BLOCK
    spec_t = pl.BlockSpec((None, None, dh, t), lambda i, j: (i, j, 0, 0))
    return pl.pallas_call(
        functools.partial(_sb_seq_kernel, nq=nq),
        grid=(b, h),
        in_specs=[spec_t, pl.BlockSpec((None, None, t, dh), lambda i, j: (i, j, 0, 0)), spec_t],
        out_specs=spec_t,
        out_shape=jax.ShapeDtypeStruct((b, h, dh, t), BF16),
        scratch_shapes=[pltpu.VMEM((SB_BLOCK, SB_BLOCK), F32),
                        pltpu.VMEM((SB_BLOCK, SB_BLOCK), F32),
                        pltpu.VMEM((SB_BLOCK, SB_BLOCK), BF16)],
        compiler_params=_cparams("parallel", "parallel"),
        name="sb_seq",
    )(q_t, k_p, v_tp)


def _sb_step_kernel(pt_ref, q_ref, k_ref, v_ref, o_ref, qb_ref, acc_ref, carry_ref, *, heads):
    p = pl.program_id(1)
    d = heads * SB_HEAD_DIM
    head_of_lane = lax.broadcasted_iota(jnp.int32, (heads, d), 1) // SB_HEAD_DIM
    head_of_row = lax.broadcasted_iota(jnp.int32, (heads, d), 0)
    own = head_of_lane == head_of_row

    @pl.when(p == 0)
    def _():
        q = jnp.broadcast_to(q_ref[...], (heads, d)) * (SB_HEAD_DIM ** -0.5)
        qb_ref[...] = jnp.where(own, q, 0.0).astype(BF16)
        acc_ref[...] = jnp.zeros_like(acc_ref)
        carry_ref[...] = jnp.zeros_like(carry_ref)

    z = _dot_nt(qb_ref[...], k_ref[...].astype(BF16))
    sp = _softplus(z)
    row = lax.broadcasted_iota(jnp.int32, (PAGE, PAGE), 0)
    col = lax.broadcasted_iota(jnp.int32, (PAGE, PAGE), 1)
    suffix01 = jnp.where(row >= col, 1.0, 0.0).astype(BF16)
    cum = _dot_exact_rhs(sp, suffix01)
    a = jnp.exp(z - cum - carry_ref[...])
    acc_ref[...] += _dot(a.astype(BF16), v_ref[...].astype(BF16))
    carry_ref[...] += jnp.sum(sp, axis=1, keepdims=True)

    @pl.when(p == pl.num_programs(1) - 1)
    def _():
        o_ref[...] = jnp.sum(jnp.where(own, acc_ref[...], 0.0), axis=0, keepdims=True)


def _sb_step(q, page_table, cache_k, cache_v, layer):
    b, d = q.shape
    n_pages = page_table.shape[1]
    heads = d // SB_HEAD_DIM
    page_spec = pl.BlockSpec((None, None, PAGE, d),
                             lambda i, p, pt: (layer, pt[i, n_pages - 1 - p], 0, 0))
    row_spec = pl.BlockSpec((None, 1, d), lambda i, p, pt: (i, 0, 0))
    grid_spec = pltpu.PrefetchScalarGridSpec(
        num_scalar_prefetch=1,
        grid=(b, n_pages),
        in_specs=[row_spec, page_spec, page_spec],
        out_specs=row_spec,
        scratch_shapes=[pltpu.VMEM((heads, d), BF16), pltpu.VMEM((heads, d), F32),
                        pltpu.VMEM((heads, 1), F32)],
    )
    out = pl.pallas_call(
        functools.partial(_sb_step_kernel, heads=heads),
        grid_spec=grid_spec,
        out_shape=jax.ShapeDtypeStruct((b, 1, d), F32),
        compiler_params=_cparams("parallel", "arbitrary"),
        name="sb_step",
    )(page_table, q.reshape(b, 1, d), cache_k, cache_v)
    return out.reshape(b, d)


def _gmlp_kernel(*refs, single, half, tm):
    if single:
        (x_ref, g_ref, win_ref, bin_ref, lng_ref, lnb_ref, wsv_ref, bsv_ref, wo_ref,
         o_ref, vout_ref, vn_ref, acc_ref) = refs
    else:
        (x_ref, g_ref, win_ref, bin_ref, lng_ref, lnb_ref, ws_ref, bst_ref, wo_ref,
         o_ref, vn_ref, acc_ref) = refs
    gd = half // CM_GROUPS
    pair = 2 * gd
    x = x_ref[...]
    xn = _rms(x, g_ref[...]).astype(BF16)

    v = _gelu(_dot(xn, win_ref[:, half:]) + bin_ref[:, half:])
    mu = jnp.mean(v, axis=-1, keepdims=True)
    var = jnp.mean(jnp.square(v - mu), axis=-1, keepdims=True)
    vn = (v - mu) * lax.rsqrt(var + EPS) * lng_ref[...] + lnb_ref[...]
    if single:
        vout_ref[...] = vn
        vn_ref[...] = vn
    else:
        vn_ref[...] = vn.astype(BF16)
        row = lax.broadcasted_iota(jnp.int32, (CM_CHUNK, CM_CHUNK), 0)
        col = lax.broadcasted_iota(jnp.int32, (CM_CHUNK, CM_CHUNK), 1)
        causal = col <= row

    for gp in range(CM_GROUPS // 2):
        lo = gp * pair
        u = _gelu(_dot(xn, win_ref[:, lo:lo + pair]) + bin_ref[:, lo:lo + pair])
        if single:
            s = vn_ref[:, lo:lo + pair] * wsv_ref[:, lo:lo + pair] + bsv_ref[:, lo:lo + pair]
        else:
            cols = []
            for gi in range(2):
                g = 2 * gp + gi
                ws = jnp.where(causal, ws_ref[g], 0.0).astype(BF16)
                bs = bst_ref[:, g:g + 1]
                rows = []
                for c in range(tm // CM_CHUNK):
                    vc = vn_ref[c * CM_CHUNK:(c + 1) * CM_CHUNK, g * gd:(g + 1) * gd]
                    rows.append(_dot(ws, vc) + bs)
                cols.append(jnp.concatenate(rows, axis=0) if len(rows) > 1 else rows[0])
            s = jnp.concatenate(cols, axis=1)
        y = (u * s).astype(BF16)
        contrib = _dot(y, wo_ref[lo:lo + pair, :])
        if gp == 0:
            acc_ref[...] = contrib
        else:
            acc_ref[...] += contrib
    o_ref[...] = x + acc_ref[...]


def _gmlp(r, g, w_in, b_in, ln_g, ln_b, w_s, b_s, w_o, single):
    m, d = r.shape
    half = w_o.shape[0]
    gd = half // CM_GROUPS
    tm = min(m, 128) if single else 256
    const = lambda i: (0, 0)
    specs = [
        pl.BlockSpec((tm, d), lambda i: (i, 0)),
        pl.BlockSpec((1, d), const),
        pl.BlockSpec((d, 2 * half), const),
        pl.BlockSpec((1, 2 * half), const),
        pl.BlockSpec((1, half), const),
        pl.BlockSpec((1, half), const),
    ]
    args = [r, g.reshape(1, d), w_in, b_in.reshape(1, -1), ln_g.reshape(1, -1), ln_b.reshape(1, -1)]
    if single:
        args += [jnp.repeat(w_s[:, 0, 0], gd).reshape(1, half), jnp.repeat(b_s[:, 0], gd).reshape(1, half)]
        specs += [pl.BlockSpec((1, half), const), pl.BlockSpec((1, half), const)]
    else:
        args += [w_s, jnp.transpose(b_s)]
        specs += [pl.BlockSpec((CM_GROUPS, CM_CHUNK, CM_CHUNK), lambda i: (0, 0, 0)),
                  pl.BlockSpec((CM_CHUNK, CM_GROUPS), const)]
    args.append(w_o)
    specs.append(pl.BlockSpec((half, d), const))
    row_out = pl.BlockSpec((tm, d), lambda i: (i, 0))
    if single:
        out_specs = [row_out, pl.BlockSpec((tm, half), lambda i: (i, 0))]
        out_shape = [jax.ShapeDtypeStruct((m, d), F32), jax.ShapeDtypeStruct((m, half), F32)]
        scratch = [pltpu.VMEM((tm, half), F32), pltpu.VMEM((tm, d), F32)]
    else:
        out_specs = row_out
        out_shape = jax.ShapeDtypeStruct((m, d), F32)
        scratch = [pltpu.VMEM((tm, half), BF16), pltpu.VMEM((tm, d), F32)]
    return pl.pallas_call(
        functools.partial(_gmlp_kernel, single=single, half=half, tm=tm),
        grid=(m // tm,),
        in_specs=specs,
        out_specs=out_specs,
        out_shape=out_shape,
        scratch_shapes=scratch,
        compiler_params=_cparams("parallel"),
        name="gmlp_step" if single else "gmlp_seq",
    )(*args)


def _ml_qkv_tail(xc, x, wq_ref, wk_ref, wv_ref, q_ref, k_ref, v_ref, scale):
    xc = xc.astype(BF16)
    q_ref[...] = _dot(xc, wq_ref[...]).astype(q_ref.dtype)
    k_ref[...] = (_dot(xc, wk_ref[...]) * scale).astype(k_ref.dtype)
    v_ref[...] = _dot(x.astype(BF16), wv_ref[...]).astype(v_ref.dtype)


def _ml_qkv_seq_kernel(x_ref, halo_ref, cw_ref, cb_ref, wq_ref, wk_ref, wv_ref, q_ref, k_ref, v_ref, xs_ref,
                       *, tiles_per_seq, tm, scale):
    i = pl.program_id(1)
    x = x_ref[...]
    xs_ref[:ML_HALO, :] = jnp.where(i % tiles_per_seq == 0, 0.0, halo_ref[...])
    xs_ref[ML_HALO:, :] = x
    y = cb_ref[...]
    for t in range(ML_CONV - 1):
        off = ML_HALO - (ML_CONV - 1) + t
        y = y + cw_ref[t:t + 1, :] * xs_ref[off:off + tm, :]
    y = y + cw_ref[ML_CONV - 1:ML_CONV, :] * x
    _ml_qkv_tail(_silu(y), x, wq_ref, wk_ref, wv_ref, q_ref, k_ref, v_ref, scale)


def _ml_qkv_step_kernel(x_ref, b0_ref, b1_ref, b2_ref, cw_ref, cb_ref, wq_ref, wk_ref, wv_ref,
                        q_ref, k_ref, v_ref, *, scale):
    x = x_ref[...]
    y = (cb_ref[...] + cw_ref[0:1, :] * b0_ref[...] + cw_ref[1:2, :] * b1_ref[...]
         + cw_ref[2:3, :] * b2_ref[...] + cw_ref[3:4, :] * x)
    _ml_qkv_tail(_silu(y), x, wq_ref, wk_ref, wv_ref, q_ref, k_ref, v_ref, scale)


def _ml_qkv(proj, conv_w, conv_b, w_q, w_k, w_v, *, seq_len=None, bufs=None, qv_dtype=BF16):
    m = proj.shape[0]
    nh, hd, _ = w_q.shape
    inner = nh * hd
    scale = hd ** -0.5
    cb = conv_b.reshape(1, inner)
    wspec = pl.BlockSpec((None, hd, hd), lambda h, i: (h, 0, 0))
    cwspec = pl.BlockSpec((ML_CONV, hd), lambda h, i: (0, h))
    cbspec = pl.BlockSpec((1, hd), lambda h, i: (0, h))
    if bufs is None:
        tm = _pick_tile(seq_len, (512, 256, 128))
        hb = tm // ML_HALO
        xspec = pl.BlockSpec((tm, hd), lambda h, i: (i, h))
        kern = functools.partial(_ml_qkv_seq_kernel, tiles_per_seq=seq_len // tm, tm=tm, scale=scale)
        args = [proj, proj, conv_w, cb, w_q, w_k, w_v]
        specs = [xspec, pl.BlockSpec((ML_HALO, hd), lambda h, i: (jnp.maximum(i * hb - 1, 0), h)),
                 cwspec, cbspec, wspec, wspec, wspec]
        scratch = [pltpu.VMEM((tm + ML_HALO, hd), F32)]
    else:
        tm = m
        xspec = pl.BlockSpec((tm, hd), lambda h, i: (i, h))
        kern = functools.partial(_ml_qkv_step_kernel, scale=scale)
        args = [proj, *bufs, conv_w, cb, w_q, w_k, w_v]
        specs = [xspec, xspec, xspec, xspec, cwspec, cbspec, wspec, wspec, wspec]
        scratch = []
    return pl.pallas_call(
        kern,
        grid=(nh, m // tm),
        in_specs=specs,
        out_specs=[xspec, xspec, xspec],
        out_shape=[jax.ShapeDtypeStruct((m, inner), qv_dtype), jax.ShapeDtypeStruct((m, inner), F32),
                   jax.ShapeDtypeStruct((m, inner), qv_dtype)],
        scratch_shapes=scratch,
        compiler_params=_cparams("parallel", "parallel"),
        name="ml_qkv",
    )(*args)


def _lane_select(x, idx):
    lane = lax.broadcasted_iota(jnp.int32, x.shape, 1)
    return jnp.sum(jnp.where(lane == idx, x, 0.0), axis=1, keepdims=True)


def _ml_chunk_kernel(q_ref, k_ref, v_ref, gt_ref, h_ref, c_out, n_out, m_out, c_ref, n_ref, m_ref, *, heads):
    hidx = pl.program_id(1)
    c = pl.program_id(2)
    L = ML_CHUNK

    @pl.when(c == 0)
    def _():
        c_ref[...] = jnp.zeros_like(c_ref)
        n_ref[...] = jnp.zeros_like(n_ref)
        m_ref[...] = jnp.zeros_like(m_ref)

    gt = gt_ref[...]
    i_col = _lane_select(gt, hidx)
    f_col = _log_sigmoid(_lane_select(gt, hidx + heads))
    row = lax.broadcasted_iota(jnp.int32, (L, L), 0)
    col = lax.broadcasted_iota(jnp.int32, (L, L), 1)
    causal = col <= row
    lower01 = jnp.where(causal, 1.0, 0.0).astype(BF16)
    ones01 = jnp.ones((L, L), BF16)
    f_b = jnp.broadcast_to(f_col, (L, L))
    i_b = jnp.broadcast_to(i_col, (L, L))
    b_c = _dot_exact_lhs(lower01, f_b)
    b_r = _dot_exact_lhs(ones01, jnp.where(row <= col, f_b, 0.0))
    i_r = _dot_exact_lhs(ones01, jnp.where(row == col, i_b, 0.0))
    b_col = b_c[:, 0:1]
    m_prev = m_ref[0:1, 0:1]

    d_log = jnp.where(causal, b_c - b_r + i_r, -jnp.inf)
    inter = b_col + m_prev
    m_t = jnp.maximum(inter, jnp.max(d_log, axis=1, keepdims=True))
    w_intra = jnp.exp(d_log - m_t)
    w_inter = jnp.exp(inter - m_t)

    q = q_ref[...].astype(BF16)
    k = k_ref[...]
    kb = k.astype(BF16)
    v = v_ref[...].astype(BF16)
    c_old = c_ref[...]
    n_old = n_ref[...]
    qk = _dot_nt(q, kb) * w_intra
    num = w_inter * _dot(q, c_old.astype(BF16)) + _dot(qk.astype(BF16), v)
    qn = _dot_nt(q, n_old.astype(BF16))[:, 0:1]
    den = w_inter * qn + jnp.sum(qk, axis=1, keepdims=True)
    h_ref[...] = num / jnp.maximum(jnp.abs(den), jnp.exp(-m_t))

    m_new = m_t[L - 1:L, :]
    b_last = b_col[L - 1:L, :]
    w_state = jnp.exp(b_last + m_prev - m_new)
    w_rows_c = jnp.exp(b_last - b_col + i_col - m_new)
    w_rows_r = jnp.exp(b_last - b_r[0:SUBLANES, :] + i_r[0:SUBLANES, :] - m_new)
    kw = (k * w_rows_c).astype(BF16)
    c_ref[...] = w_state * c_old + _dot_tn(kw, v)
    n_ref[...] = w_state * n_old + _dot(w_rows_r.astype(BF16), kb)
    m_ref[...] = jnp.broadcast_to(m_new, m_ref.shape)

    @pl.when(c == pl.num_programs(2) - 1)
    def _():
        c_out[...] = c_ref[...]
        n_out[...] = n_ref[0:1, :]
        m_out[...] = m_ref[0:1, :]


def _ml_chunks(q, k, v, gates, batch, seq_len, heads):
    m, inner = q.shape
    hd = inner // heads
    nc = seq_len // ML_CHUNK
    xspec = pl.BlockSpec((ML_CHUNK, hd), lambda b, h, c: (b * nc + c, h))
    return pl.pallas_call(
        functools.partial(_ml_chunk_kernel, heads=heads),
        grid=(batch, heads, nc),
        in_specs=[xspec, xspec, xspec, pl.BlockSpec((ML_CHUNK, LANES), lambda b, h, c: (b * nc + c, 0))],
        out_specs=[xspec,
                   pl.BlockSpec((None, None, hd, hd), lambda b, h, c: (b, h, 0, 0)),
                   pl.BlockSpec((None, None, 1, hd), lambda b, h, c: (b, h, 0, 0)),
                   pl.BlockSpec((None, None, 1, LANES), lambda b, h, c: (b, h, 0, 0))],
        out_shape=[jax.ShapeDtypeStruct((m, inner), F32),
                   jax.ShapeDtypeStruct((batch, heads, hd, hd), F32),
                   jax.ShapeDtypeStruct((batch, heads, 1, hd), F32),
                   jax.ShapeDtypeStruct((batch, heads, 1, LANES), F32)],
        scratch_shapes=[pltpu.VMEM((hd, hd), F32), pltpu.VMEM((SUBLANES, hd), F32),
                        pltpu.VMEM((SUBLANES, LANES), F32)],
        compiler_params=_cparams("parallel", "parallel", "arbitrary"),
        name="ml_chunks",
    )(q, k, v, gates)


def _ml_step_kernel(q_ref, k_ref, v_ref, gt_ref, mp_ref, c_ref, n_ref, h_ref, c_out, n_out, m_out, *, heads):
    hidx = pl.program_id(1)
    gt = gt_ref[...]
    i_g = _lane_select(gt, hidx)
    f_g = _log_sigmoid(_lane_select(gt, hidx + heads))
    m_prev = _lane_select(mp_ref[...], hidx)
    inter = f_g + m_prev
    m_t = jnp.maximum(inter, i_g)
    w_in = jnp.exp(i_g - m_t)
    w_st = jnp.exp(inter - m_t)

    q = q_ref[...]
    k = k_ref[...]
    v = v_ref[...]
    n = n_ref[...]
    c_old = c_ref[...]
    hd = q.shape[1]
    q8 = jnp.broadcast_to(q, (SUBLANES, hd)).astype(BF16)
    q_c = _dot(q8, c_old.astype(BF16))[0:1, :]
    qk = jnp.sum(q * k, axis=1, keepdims=True) * w_in
    num = w_st * q_c + qk * v
    den = w_st * jnp.sum(q * n, axis=1, keepdims=True) + qk
    h_ref[...] = num / jnp.maximum(jnp.abs(den), jnp.exp(-m_t))

    kw = k * w_in
    first_row = lax.broadcasted_iota(jnp.int32, (SUBLANES, hd), 0) == 0
    kw8 = jnp.where(first_row, jnp.broadcast_to(kw, (SUBLANES, hd)), 0.0).astype(BF16)
    v8 = jnp.broadcast_to(v, (SUBLANES, hd)).astype(BF16)
    c_out[...] = w_st * c_old + _dot_tn(kw8, v8)
    n_out[...] = w_st * n + kw
    m_out[...] = jnp.broadcast_to(m_t, m_out.shape)


def _ml_step(q, k, v, gates, m_prev, c_state, n_state, heads):
    b, inner = q.shape
    hd = inner // heads
    r4 = lambda a: a.reshape(b, heads, 1, hd)
    vspec = pl.BlockSpec((None, None, 1, hd), lambda i, h: (i, h, 0, 0))
    gspec = pl.BlockSpec((None, 1, LANES), lambda i, h: (i, 0, 0))
    cspec = pl.BlockSpec((None, None, hd, hd), lambda i, h: (i, h, 0, 0))
    mspec = pl.BlockSpec((None, None, 1, LANES), lambda i, h: (i, h, 0, 0))
    return pl.pallas_call(
        functools.partial(_ml_step_kernel, heads=heads),
        grid=(b, heads),
        in_specs=[vspec, vspec, vspec, gspec, gspec, cspec, vspec],
        out_specs=[vspec, cspec, vspec, mspec],
        out_shape=[jax.ShapeDtypeStruct((b, heads, 1, hd), F32),
                   jax.ShapeDtypeStruct((b, heads, hd, hd), F32),
                   jax.ShapeDtypeStruct((b, heads, 1, hd), F32),
                   jax.ShapeDtypeStruct((b, heads, 1, LANES), F32)],
        compiler_params=_cparams("parallel", "parallel"),
        name="ml_step",
    )(r4(q), r4(k), r4(v), gates.reshape(b, 1, LANES), m_prev.reshape(b, 1, LANES), c_state, r4(n_state))


def _ml_out_kernel(h_ref, o_ref, ng_ref, wo_ref, r_ref, out_ref, *, heads):
    h = h_ref[...]
    hd = h.shape[1] // heads
    parts = []
    for j in range(heads):
        hh = h[:, j * hd:(j + 1) * hd]
        mu = jnp.mean(hh, axis=-1, keepdims=True)
        var = jnp.mean(jnp.square(hh - mu), axis=-1, keepdims=True)
        parts.append((hh - mu) * lax.rsqrt(var + EPS))
    hn = jnp.concatenate(parts, axis=1) * ng_ref[...]
    out = (_sigmoid(o_ref[...]) * hn).astype(BF16)
    out_ref[...] = r_ref[...] + _dot(out, wo_ref[...])


def _ml_out(h, proj, norm_g, w_o, r, heads):
    m, inner = h.shape
    d = r.shape[1]
    tm = _pick_tile(m, (256, 128))
    return pl.pallas_call(
        functools.partial(_ml_out_kernel, heads=heads),
        grid=(m // tm,),
        in_specs=[pl.BlockSpec((tm, inner), lambda i: (i, 0)),
                  pl.BlockSpec((tm, inner), lambda i: (i, 1)),
                  pl.BlockSpec((1, inner), lambda i: (0, 0)),
                  pl.BlockSpec((inner, d), lambda i: (0, 0)),
                  pl.BlockSpec((tm, d), lambda i: (i, 0))],
        out_specs=pl.BlockSpec((tm, d), lambda i: (i, 0)),
        out_shape=jax.ShapeDtypeStruct((m, d), F32),
        compiler_params=_cparams("parallel"),
        name="ml_out",
    )(h, proj, norm_g.reshape(1, inner), w_o, r)


def _permute_keys(x, t):
    b, _, h, dh = x.shape
    return x.reshape(b, t // SB_BLOCK, SUBLANES, SB_SEG, h, dh)


def _trunk(x, p, seq_len, weights, *, past=None, ml_state=None, ffn_state=None):
    (norm_mix, norm_ffn, norm_ple, norm_final, sb_w_qkv, sb_w_o, cm_w_in, cm_b_in, cm_ln_g, cm_ln_b,
     cm_w_s, cm_b_s, cm_w_o, ml_w_in, ml_b_gates, ml_conv_w, ml_conv_b, ml_w_q, ml_w_k, ml_w_v,
     ml_norm_g, ml_w_o, ffn_w_up, ffn_conv_w, ffn_conv_b, ffn_w_down, ple_w_proj, ple_w_gate) = weights
    m, d = x.shape
    batch = m // seq_len
    depth = norm_mix.shape[0]
    step = seq_len == 1
    heads = d // SB_HEAD_DIM
    outs = dict(k=[], v=[], cm=[], c=[], n=[], m=[], conv=[], ffn=[])
    r = x
    for i in range(depth):
        kind, j = i % 3, i // 3
        if kind == 0:
            qkv = _matmul(r, sb_w_qkv[j], g=norm_mix[i], name="sb_qkv")
            q, k, v = qkv[:, :d], qkv[:, d:2 * d], qkv[:, 2 * d:]
            outs["k"].append(k.reshape(batch, seq_len, heads, SB_HEAD_DIM))
            outs["v"].append(v.reshape(batch, seq_len, heads, SB_HEAD_DIM))
            if step:
                page_table, cache_k, cache_v = past
                o = _sb_step(q, page_table, cache_k, cache_v, j)
            else:
                q_t = q.astype(BF16).reshape(batch, seq_len, heads, SB_HEAD_DIM).transpose(0, 2, 3, 1)
                k_p = _permute_keys(k.astype(BF16).reshape(batch, seq_len, heads, SB_HEAD_DIM), seq_len)
                k_p = k_p.transpose(0, 4, 1, 3, 2, 5).reshape(batch, heads, seq_len, SB_HEAD_DIM)
                v_p = _permute_keys(v.astype(BF16).reshape(batch, seq_len, heads, SB_HEAD_DIM), seq_len)
                v_p = v_p.transpose(0, 4, 5, 1, 3, 2).reshape(batch, heads, SB_HEAD_DIM, seq_len)
                o_t = _sb_seq(q_t, k_p, v_p)
                o = o_t.transpose(0, 3, 1, 2).reshape(m, d)
            r = _matmul(o, sb_w_o[j], res=r, name="sb_out")
        elif kind == 1:
            res = _gmlp(r, norm_mix[i], cm_w_in[j], cm_b_in[j], cm_ln_g[j], cm_ln_b[j], cm_w_s[j], cm_b_s[j],
                        cm_w_o[j], step)
            if step:
                r, vn = res
                outs["cm"].append(vn.reshape(batch, 1, -1))
            else:
                r = res
        else:
            nh = ML_HEADS
            inner = ml_w_q.shape[2] * nh
            w_main = ml_w_in[j][:, :2 * inner]
            w_gate = jnp.pad(ml_w_in[j][:, 2 * inner:], ((0, 0), (0, LANES - 2 * nh)))
            b_gate = jnp.pad(ml_b_gates[j], (0, LANES - 2 * nh))
            proj = _matmul(r, w_main, g=norm_mix[i], name="ml_in")
            gates = _matmul(r, w_gate, g=norm_mix[i], bias=b_gate, name="ml_gates")
            xm = proj[:, :inner]
            if step:
                c0, n0, m0, conv0 = ml_state
                bufs = [conv0[j][:, t] for t in range(ML_CONV - 1)]
                q, k, v = _ml_qkv(proj, ml_conv_w[j], ml_conv_b[j], ml_w_q[j], ml_w_k[j], ml_w_v[j],
                                  bufs=bufs, qv_dtype=F32)
                m_prev = jnp.pad(m0[j], ((0, 0), (0, LANES - nh)))
                hh, c_new, n_new, m_new = _ml_step(q, k, v, gates, m_prev, c0[j], n0[j], nh)
                hh = hh.reshape(m, inner)
                outs["conv"].append(jnp.concatenate([conv0[j][:, 1:], xm[:, None, :]], axis=1))
                n_new = n_new.reshape(batch, nh, -1)
            else:
                q, k, v = _ml_qkv(proj, ml_conv_w[j], ml_conv_b[j], ml_w_q[j], ml_w_k[j], ml_w_v[j],
                                  seq_len=seq_len)
                hh, c_new, n_new, m_new = _ml_chunks(q, k, v, gates, batch, seq_len, nh)
                outs["conv"].append(xm.reshape(batch, seq_len, inner)[:, seq_len - (ML_CONV - 1):])
                n_new = n_new.reshape(batch, nh, -1)
            outs["c"].append(c_new)
            outs["n"].append(n_new)
            outs["m"].append(m_new[:, :, 0, 0])
            r = _ml_out(hh, proj, ml_norm_g[j], ml_w_o[j], r, nh)

        if step:
            a = _matmul(r, ffn_w_up[i], g=norm_ffn[i], name="ffn_up")
            buf = ffn_state[i]
            r = _ffn_step(r, a, buf[:, 0], buf[:, 1], ffn_conv_w[i], ffn_conv_b[i], ffn_w_down[i])
            outs["ffn"].append(jnp.concatenate([buf[:, 1:], a[:, None, :]], axis=1))
        else:
            tail = r.reshape(batch, seq_len, d)[:, seq_len - (FFN_CONV - 1):].reshape(-1, d)
            a_tail = _matmul(tail, ffn_w_up[i], g=norm_ffn[i], name="ffn_tail")
            outs["ffn"].append(a_tail.reshape(batch, FFN_CONV - 1, -1))
            r = _ffn_seq(r, norm_ffn[i], ffn_w_up[i], ffn_conv_w[i], ffn_conv_b[i], ffn_w_down[i], seq_len)

        r = _ple(r, norm_ple[i], ple_w_gate[i], p[i], ple_w_proj[i],
                 g_final=norm_final if i == depth - 1 else None)
    return r, outs


def kernel(x_prompt, x_sample, p_prompt, p_sample, page_table, cache_k, cache_v, state_mlstm_c, state_mlstm_n, state_mlstm_m, state_mlstm_conv, state_ffn_conv, norm_mix, norm_ffn, norm_ple, norm_final, sb_w_qkv, sb_w_o, cm_w_in, cm_b_in, cm_ln_g, cm_ln_b, cm_w_s, cm_b_s, cm_w_o, ml_w_in, ml_b_gates, ml_conv_w, ml_conv_b, ml_w_q, ml_w_k, ml_w_v, ml_norm_g, ml_w_o, ffn_w_up, ffn_conv_w, ffn_conv_b, ffn_w_down, ple_w_proj, ple_w_gate):
    bf = lambda w: w.astype(BF16)
    weights = (norm_mix, norm_ffn, norm_ple, norm_final, bf(sb_w_qkv), bf(sb_w_o), bf(cm_w_in), cm_b_in,
               cm_ln_g, cm_ln_b, cm_w_s, cm_b_s, bf(cm_w_o), bf(ml_w_in), ml_b_gates, ml_conv_w, ml_conv_b,
               bf(ml_w_q), bf(ml_w_k), bf(ml_w_v), ml_norm_g, bf(ml_w_o), bf(ffn_w_up), ffn_conv_w,
               ffn_conv_b, bf(ffn_w_down), bf(ple_w_proj), bf(ple_w_gate))
    b, t, d = x_prompt.shape
    bs, ts, _ = x_sample.shape
    depth = p_prompt.shape[0]
    n_sb, n_phys, page, sbh, sbd = cache_k.shape

    y_p, o_p = _trunk(x_prompt.reshape(b * t, d), p_prompt.reshape(depth, b * t, -1), t, weights)
    y_s, o_s = _trunk(
        x_sample.reshape(bs * ts, d), p_sample.reshape(depth, bs * ts, -1), ts, weights,
        past=(page_table, cache_k.reshape(n_sb, n_phys, page, sbh * sbd), cache_v.reshape(n_sb, n_phys, page, sbh * sbd)),
        ml_state=(state_mlstm_c, state_mlstm_n, state_mlstm_m, state_mlstm_conv),
        ffn_state=state_ffn_conv)

    st = jnp.stack
    return (y_p.reshape(b, t, d), y_s.reshape(bs, ts, d),
            st(o_p["k"]), st(o_p["v"]), st(o_s["k"]), st(o_s["v"]), st(o_s["cm"]),
            st(o_p["c"]), st(o_p["n"]), st(o_p["m"]), st(o_p["conv"]),
            st(o_s["c"]), st(o_s["n"]), st(o_s["m"]), st(o_s["conv"]),
            st(o_p["ffn"]), st(o_s["ffn"]))
```

```python
import functools

import jax
import jax.numpy as jnp
from jax import lax
from jax.experimental import pallas as pl
from jax.experimental.pallas import tpu as pltpu

F32 = jnp.float32
BF16 = jnp.bfloat16
EPS = 1e-6

VMEM_LIMIT_BYTES = 56 * 1024 * 1024
LANES = 128
SUBLANES = 8

SB_HEAD_DIM = 64
SB_BLOCK = 256
SB_SEG = SB_BLOCK // SUBLANES
SB_HEADS_PER_STEP = 2
SB_PAGES_PER_STEP = 8
PAGE = 128
CM_GROUPS = 8
CM_CHUNK = 128
ML_HEADS = 4
ML_CHUNK = 128
ML_CONV = 4
FFN_CONV = 3
FFN_HALO = 16
ML_HALO = 8


def _cparams(*sem):
    return pltpu.CompilerParams(dimension_semantics=sem, vmem_limit_bytes=VMEM_LIMIT_BYTES)


def _dot(a, b):
    return jnp.dot(a, b, preferred_element_type=F32)


def _dot_nt(a, b):
    return lax.dot_general(a, b, (((1,), (1,)), ((), ())), preferred_element_type=F32)


def _dot_tn(a, b):
    return lax.dot_general(a, b, (((0,), (0,)), ((), ())), preferred_element_type=F32)


def _rms(x, g):
    ms = jnp.mean(x * x, axis=-1, keepdims=True)
    return x * lax.rsqrt(ms + EPS) * g


def _sigmoid(x):
    return 1.0 / (1.0 + jnp.exp(-x))


def _silu(x):
    return x * _sigmoid(x)


def _gelu(x):
    return 0.5 * x * (1.0 + jnp.tanh(0.7978845608028654 * (x + 0.044715 * (x * x * x))))


def _softplus(x):
    return jnp.maximum(x, 0.0) + jnp.log(1.0 + jnp.exp(-jnp.abs(x)))


def _log_sigmoid(x):
    return jnp.minimum(x, 0.0) - jnp.log(1.0 + jnp.exp(-jnp.abs(x)))


def _split3(x):
    hi = x.astype(BF16)
    r1 = x - hi.astype(F32)
    mid = r1.astype(BF16)
    lo = (r1 - mid.astype(F32)).astype(BF16)
    return hi, mid, lo


def _dot_exact_lhs(lhs01, x):
    hi, mid, lo = _split3(x)
    return _dot(lhs01, hi) + _dot(lhs01, mid) + _dot(lhs01, lo)


def _dot_exact_rhs(x, rhs01):
    hi, mid, lo = _split3(x)
    return _dot(hi, rhs01) + _dot(mid, rhs01) + _dot(lo, rhs01)


def _pick_tile(n, candidates):
    for c in candidates:
        if n % c == 0:
            return c
    return n


def _mm_kernel(*refs, norm, act, has_bias, has_res):
    it = iter(refs)
    x_ref = next(it)
    g_ref = next(it) if norm else None
    w_ref = next(it)
    b_ref = next(it) if has_bias else None
    r_ref = next(it) if has_res else None
    o_ref = next(it)
    xn_ref = next(it)

    @pl.when(pl.program_id(1) == 0)
    def _():
        x = x_ref[...].astype(F32)
        if norm:
            x = _rms(x, g_ref[...])
        xn_ref[...] = x.astype(BF16)

    acc = _dot(xn_ref[...], w_ref[...])
    if has_bias:
        acc = acc + b_ref[...]
    if act == "gelu":
        acc = _gelu(acc)
    if has_res:
        acc = acc + r_ref[...]
    o_ref[...] = acc.astype(o_ref.dtype)


def _matmul(x, w, *, g=None, bias=None, res=None, act=None, out_dtype=F32, name="matmul"):
    m, k = x.shape
    n = w.shape[1]
    tm = _pick_tile(m, (512, 256, 128, 16, 8))
    tn = _pick_tile(n, (512, 384, 256, 128))
    norm = g is not None
    args = [x]
    specs = [pl.BlockSpec((tm, k), lambda i, j: (i, 0))]
    if norm:
        args.append(g.reshape(1, k))
        specs.append(pl.BlockSpec((1, k), lambda i, j: (0, 0)))
    args.append(w)
    specs.append(pl.BlockSpec((k, tn), lambda i, j: (0, j)))
    if bias is not None:
        args.append(bias.reshape(1, n))
        specs.append(pl.BlockSpec((1, tn), lambda i, j: (0, j)))
    if res is not None:
        args.append(res)
        specs.append(pl.BlockSpec((tm, tn), lambda i, j: (i, j)))
    kern = functools.partial(_mm_kernel, norm=norm, act=act, has_bias=bias is not None,
                             has_res=res is not None)
    return pl.pallas_call(
        kern,
        grid=(m // tm, n // tn),
        in_specs=specs,
        out_specs=pl.BlockSpec((tm, tn), lambda i, j: (i, j)),
        out_shape=jax.ShapeDtypeStruct((m, n), out_dtype),
        scratch_shapes=[pltpu.VMEM((tm, k), BF16)],
        compiler_params=_cparams("parallel", "arbitrary"),
        name=name,
    )(*args)


def _ffn_seq_kernel(x_ref, halo_ref, g_ref, wg_ref, wu_ref, cwg_ref, cwu_ref, cbg_ref, cbu_ref, wd_ref,
                    o_ref, xn_ref, a_ref, c_ref, *, tiles_per_seq, tm, tf):
    i = pl.program_id(0)
    j = pl.program_id(1)

    @pl.when(j == 0)
    def _():
        g = g_ref[...]
        xn_ref[FFN_HALO:, :] = _rms(x_ref[...], g).astype(BF16)
        hal = _rms(halo_ref[...], g)
        hal = jnp.where(i % tiles_per_seq == 0, 0.0, hal)
        xn_ref[:FFN_HALO, :] = hal.astype(BF16)

    xa = xn_ref[...]
    a_ref[:, :tf] = _dot(xa, wg_ref[...])
    a_ref[:, tf:] = _dot(xa, wu_ref[...])

    def conv(lo, cw_ref, cb_ref):
        y = cb_ref[...]
        for t in range(FFN_CONV):
            off = FFN_HALO - (FFN_CONV - 1) + t
            y = y + cw_ref[t:t + 1, :] * a_ref[off:off + tm, lo:lo + tf]
        return y

    cg = conv(0, cwg_ref, cbg_ref)
    cu = conv(tf, cwu_ref, cbu_ref)
    c_ref[:, pl.ds(pl.multiple_of(j * tf, tf), tf)] = (_silu(cg) * cu).astype(BF16)

    @pl.when(j == pl.num_programs(1) - 1)
    def _():
        o_ref[...] = x_ref[...] + _dot(c_ref[...], wd_ref[...])


def _ffn_seq(r, g, w_up, conv_w, conv_b, w_down, seq_len):
    m, d = r.shape
    dff = w_down.shape[0]
    tm = _pick_tile(seq_len, (1024, 512, 256, 128))
    tf = 256
    nj = dff // tf
    hb = tm // FFN_HALO
    kern = functools.partial(_ffn_seq_kernel, tiles_per_seq=seq_len // tm, tm=tm, tf=tf)
    return pl.pallas_call(
        kern,
        grid=(m // tm, nj),
        in_specs=[
            pl.BlockSpec((tm, d), lambda i, j: (i, 0)),
            pl.BlockSpec((FFN_HALO, d), lambda i, j: (jnp.maximum(i * hb - 1, 0), 0)),
            pl.BlockSpec((1, d), lambda i, j: (0, 0)),
            pl.BlockSpec((d, tf), lambda i, j: (0, j)),
            pl.BlockSpec((d, tf), lambda i, j: (0, nj + j)),
            pl.BlockSpec((FFN_CONV, tf), lambda i, j: (0, j)),
            pl.BlockSpec((FFN_CONV, tf), lambda i, j: (0, nj + j)),
            pl.BlockSpec((1, tf), lambda i, j: (0, j)),
            pl.BlockSpec((1, tf), lambda i, j: (0, nj + j)),
            pl.BlockSpec((dff, d), lambda i, j: (0, 0)),
        ],
        out_specs=pl.BlockSpec((tm, d), lambda i, j: (i, 0)),
        out_shape=jax.ShapeDtypeStruct((m, d), F32),
        scratch_shapes=[pltpu.VMEM((tm + FFN_HALO, d), BF16),
                        pltpu.VMEM((tm + FFN_HALO, 2 * tf), F32),
                        pltpu.VMEM((tm, dff), BF16)],
        compiler_params=_cparams("parallel", "arbitrary"),
        name="ffn_seq",
    )(r, r, g.reshape(1, d), w_up, w_up, conv_w, conv_w, conv_b.reshape(1, -1), conv_b.reshape(1, -1), w_down)


def _ffn_step_kernel(ag_ref, au_ref, b0g_ref, b0u_ref, b1g_ref, b1u_ref, cwg_ref, cwu_ref, cbg_ref, cbu_ref,
                     wd_ref, r_ref, o_ref, acc_ref):
    j = pl.program_id(0)

    @pl.when(j == 0)
    def _():
        acc_ref[...] = jnp.zeros_like(acc_ref)

    def conv(a_ref, b0_ref, b1_ref, cw_ref, cb_ref):
        return (cb_ref[...] + cw_ref[0:1, :] * b0_ref[...] + cw_ref[1:2, :] * b1_ref[...]
                + cw_ref[2:3, :] * a_ref[...])

    cg = conv(ag_ref, b0g_ref, b1g_ref, cwg_ref, cbg_ref)
    cu = conv(au_ref, b0u_ref, b1u_ref, cwu_ref, cbu_ref)
    acc_ref[...] += _dot((_silu(cg) * cu).astype(BF16), wd_ref[...])

    @pl.when(j == pl.num_programs(0) - 1)
    def _():
        o_ref[...] = r_ref[...] + acc_ref[...]


def _ffn_step(r, a, buf0, buf1, conv_w, conv_b, w_down):
    m, d = r.shape
    dff = w_down.shape[0]
    tf = 256
    nj = dff // tf
    lo = lambda j: (0, j)
    hi = lambda j: (0, nj + j)
    cb = conv_b.reshape(1, -1)
    return pl.pallas_call(
        _ffn_step_kernel,
        grid=(nj,),
        in_specs=[
            pl.BlockSpec((m, tf), lo), pl.BlockSpec((m, tf), hi),
            pl.BlockSpec((m, tf), lo), pl.BlockSpec((m, tf), hi),
            pl.BlockSpec((m, tf), lo), pl.BlockSpec((m, tf), hi),
            pl.BlockSpec((FFN_CONV, tf), lo), pl.BlockSpec((FFN_CONV, tf), hi),
            pl.BlockSpec((1, tf), lo), pl.BlockSpec((1, tf), hi),
            pl.BlockSpec((tf, d), lambda j: (j, 0)),
            pl.BlockSpec((m, d), lambda j: (0, 0)),
        ],
        out_specs=pl.BlockSpec((m, d), lambda j: (0, 0)),
        out_shape=jax.ShapeDtypeStruct((m, d), F32),
        scratch_shapes=[pltpu.VMEM((m, d), F32)],
        compiler_params=_cparams("arbitrary"),
        name="ffn_step",
    )(a, a, buf0, buf0, buf1, buf1, conv_w, conv_w, cb, cb, w_down, r)


def _ple_kernel(*refs, final):
    if final:
        r_ref, g_ref, wg_ref, p_ref, wp_ref, gf_ref, o_ref = refs
    else:
        r_ref, g_ref, wg_ref, p_ref, wp_ref, o_ref = refs
    r = r_ref[...]
    xn = _rms(r, g_ref[...]).astype(BF16)
    gate = _sigmoid(_dot(xn, wg_ref[...]))
    pp = _dot(p_ref[...].astype(BF16), wp_ref[...])
    out = r + gate * pp
    if final:
        out = _rms(out, gf_ref[...])
    o_ref[...] = out


def _ple(r, g, w_gate, p, w_proj, g_final=None):
    m, d = r.shape
    pd = p.shape[1]
    tm = _pick_tile(m, (512, 256, 128))
    final = g_final is not None
    args = [r, g.reshape(1, d), w_gate, p, w_proj]
    specs = [
        pl.BlockSpec((tm, d), lambda i: (i, 0)),
        pl.BlockSpec((1, d), lambda i: (0, 0)),
        pl.BlockSpec((d, d), lambda i: (0, 0)),
        pl.BlockSpec((tm, pd), lambda i: (i, 0)),
        pl.BlockSpec((pd, d), lambda i: (0, 0)),
    ]
    if final:
        args.append(g_final.reshape(1, d))
        specs.append(pl.BlockSpec((1, d), lambda i: (0, 0)))
    return pl.pallas_call(
        functools.partial(_ple_kernel, final=final),
        grid=(m // tm,),
        in_specs=specs,
        out_specs=pl.BlockSpec((tm, d), lambda i: (i, 0)),
        out_shape=jax.ShapeDtypeStruct((m, d), F32),
        compiler_params=_cparams("parallel"),
        name="ple",
    )(*args)


def _sb_qkv_seq_kernel(*refs, heads, aliased):
    x_ref, g_ref, wqt_ref, wkt_ref, wvt_ref, wk_ref = refs[:6]
    kt_ref, vt_ref, qt_ref, vtb_ref, kh_ref = refs[6 + (2 if aliased else 0):]
    xn = _rms(x_ref[...], g_ref[...]).astype(BF16)
    qt_ref[...] = _dot_nt(wqt_ref[...], xn).astype(BF16)
    kt_ref[...] = _dot_nt(wkt_ref[...], xn)
    vt = _dot_nt(wvt_ref[...], xn)
    vt_ref[...] = vt
    vtb_ref[...] = vt.astype(BF16)
    k = _dot(xn, wk_ref[...])
    for h in range(heads):
        kh_ref[h] = k[:, h * SB_HEAD_DIM:(h + 1) * SB_HEAD_DIM].astype(BF16)


def _sb_qkv_seq(r, g, w_qkv, layer, n_layers, kt_all, vt_all, seq_len):
    m, d = r.shape
    heads = d // SB_HEAD_DIM
    tm = 256
    tps = seq_len // tm
    wq, wk, wv = w_qkv[:, :d], w_qkv[:, d:2 * d], w_qkv[:, 2 * d:]
    const = lambda i: (0, 0)
    wspec = pl.BlockSpec((d, d), const)
    aliased = kt_all is not None
    args = [r, g.reshape(1, d), wq.T, wk.T, wv.T, wk]
    specs = [pl.BlockSpec((tm, d), lambda i: (i, 0)), pl.BlockSpec((1, d), const), wspec, wspec, wspec, wspec]
    aliases = {}
    if aliased:
        args += [kt_all, vt_all]
        specs += [pl.BlockSpec(memory_space=pl.ANY), pl.BlockSpec(memory_space=pl.ANY)]
        aliases = {6: 0, 7: 1}
    layer_spec = pl.BlockSpec((None, None, d, tm), lambda i: (layer, i // tps, 0, i % tps))
    layer_shape = jax.ShapeDtypeStruct((n_layers, m // seq_len, d, seq_len), F32)
    t_spec = pl.BlockSpec((d, tm), lambda i: (0, i))
    return pl.pallas_call(
        functools.partial(_sb_qkv_seq_kernel, heads=heads, aliased=aliased),
        grid=(m // tm,),
        in_specs=specs,
        out_specs=[layer_spec, layer_spec, t_spec, t_spec,
                   pl.BlockSpec((heads, tm, SB_HEAD_DIM), lambda i: (0, i, 0))],
        out_shape=[layer_shape, layer_shape,
                   jax.ShapeDtypeStruct((d, m), BF16), jax.ShapeDtypeStruct((d, m), BF16),
                   jax.ShapeDtypeStruct((heads, m, SB_HEAD_DIM), BF16)],
        input_output_aliases=aliases,
        compiler_params=_cparams("parallel"),
        name="sb_qkv_seq",
    )(*args)


def _sublane_suffix_exclusive(x):
    idx = lax.broadcasted_iota(jnp.int32, x.shape, 0)
    y = jnp.where(idx + 1 < SUBLANES, pltpu.roll(x, SUBLANES - 1, 0), 0.0)
    y = y + jnp.where(idx + 1 < SUBLANES, pltpu.roll(y, SUBLANES - 1, 0), 0.0)
    y = y + jnp.where(idx + 2 < SUBLANES, pltpu.roll(y, SUBLANES - 2, 0), 0.0)
    y = y + jnp.where(idx + 4 < SUBLANES, pltpu.roll(y, SUBLANES - 4, 0), 0.0)
    return y


def _sb_seq_kernel(q_ref, k_ref, v_ref, o_ref, kp_ref, vp_ref, z_ref, e_ref, a_ref, f_ref, acc_ref, c_ref,
                   *, nq, hp):
    tb = SB_BLOCK
    dh = SB_HEAD_DIM
    seg_idx = lax.broadcasted_iota(jnp.int32, (SUBLANES, tb), 0) * SB_SEG
    lane_idx = lax.broadcasted_iota(jnp.int32, (SUBLANES, tb), 1)
    slot = lax.broadcasted_iota(jnp.int32, (tb, tb), 0)
    key = lax.broadcasted_iota(jnp.int32, (tb, tb), 1)
    perm01 = jnp.where(key == (slot & (SUBLANES - 1)) * SB_SEG + (slot >> 3), 1.0, 0.0).astype(BF16)

    for kt in range(nq):
        cols = slice(kt * tb, (kt + 1) * tb)
        for hs in range(hp):
            feat = slice(hs * dh, (hs + 1) * dh)
            kp_ref[hs, cols, :] = _dot(perm01, k_ref[hs, cols, :]).astype(BF16)
            vp_ref[feat, cols] = _dot_nt(v_ref[feat, cols], perm01).astype(BF16)


    def scores(qts, kj):
        k0 = pl.multiple_of(kj * tb, tb)
        return [_dot(kp_ref[hs, pl.ds(k0, tb), :], qts[hs]) for hs in range(hp)]

    def weighted_values(slot, kj):
        k0 = pl.multiple_of(kj * tb, tb)
        return [_dot(vp_ref[hs * dh:(hs + 1) * dh, pl.ds(k0, tb)], a_ref[slot, hs]) for hs in range(hp)]

    def accumulate(slot, parts):
        for hs in range(hp):
            acc_ref[hs * dh:(hs + 1) * dh, :] += parts[hs] * f_ref[slot, hs:hs + 1, :]

    def weights(slot, hs, diag):
        run = jnp.zeros((SUBLANES, tb), F32)
        for r in range(SB_SEG - 1, -1, -1):
            rows = slice(r * SUBLANES, (r + 1) * SUBLANES)
            zr = z_ref[slot, hs, rows, :]
            sp = _softplus(zr)
            if diag:
                sp = jnp.where(seg_idx + r < lane_idx, sp, 0.0)
            run = run + sp
            e_ref[hs, rows, :] = jnp.exp(zr - run)
        later = _sublane_suffix_exclusive(run)
        ecv = jnp.exp(-later)
        for r in range(0, SB_SEG, 2):
            parts = []
            for rr in (r, r + 1):
                a = e_ref[hs, rr * SUBLANES:(rr + 1) * SUBLANES, :] * ecv
                if diag:
                    a = jnp.where(seg_idx + rr < lane_idx, a, 0.0)
                parts.append(a)
            a_ref[slot, hs, r * SUBLANES:(r + 2) * SUBLANES, :] = jnp.concatenate(parts, axis=0).astype(BF16)
        return (later + run)[0:1, :]

    def q_body(qi, _):
        q0 = pl.multiple_of(qi * tb, tb)
        qts = [(q_ref[hs * dh:(hs + 1) * dh, pl.ds(q0, tb)].astype(F32) * (dh ** -0.5)).astype(BF16)
               for hs in range(hp)]
        z_first = scores(qts, qi)
        z_next = scores(qts, jnp.maximum(qi - 1, 0))
        acc_ref[...] = jnp.zeros_like(acc_ref)
        for hs in range(hp):
            z_ref[0, hs] = z_first[hs]
            z_ref[1, hs] = z_next[hs]
        for hs in range(hp):
            c_ref[hs:hs + 1, :] = weights(0, hs, True)
            f_ref[0, hs:hs + 1, :] = jnp.ones((1, tb), F32)

        def step(s, slot, prefetch=True):
            prev = 1 - slot
            parts = weighted_values(prev, qi - s + 1)
            if prefetch:
                z_next = scores(qts, jnp.maximum(qi - s - 1, 0))
                for hs in range(hp):
                    z_ref[prev, hs] = z_next[hs]
            for hs in range(hp):
                c_old = c_ref[hs:hs + 1, :]
                total = weights(slot, hs, False)
                f_ref[slot, hs:hs + 1, :] = jnp.exp(-c_old)
                c_ref[hs:hs + 1, :] = c_old + total
            accumulate(prev, parts)

        def pair_body(p, _):
            step(2 * p + 1, 1)
            step(2 * p + 2, 0)
            return 0

        lax.fori_loop(0, qi // 2, pair_body, 0)

        @pl.when(qi % 2 == 1)
        def _():
            step(qi, 1, prefetch=False)
            accumulate(1, weighted_values(1, 0))

        @pl.when(qi % 2 == 0)
        def _():
            accumulate(0, weighted_values(0, 0))

        o_ref[pl.ds(q0, tb), :] = jnp.transpose(acc_ref[...]).astype(o_ref.dtype)
        return 0

    lax.fori_loop(0, nq, q_body, 0)


def _sb_seq(q_t, k_h, v_t, batch, seq_len):
    d, m = q_t.shape
    heads = d // SB_HEAD_DIM
    hp = SB_HEADS_PER_STEP
    nq = seq_len // SB_BLOCK
    t_spec = pl.BlockSpec((hp * SB_HEAD_DIM, seq_len), lambda b, j: (j, b))
    return pl.pallas_call(
        functools.partial(_sb_seq_kernel, nq=nq, hp=hp),
        grid=(batch, heads // hp),
        in_specs=[t_spec, pl.BlockSpec((hp, seq_len, SB_HEAD_DIM), lambda b, j: (j, b, 0)), t_spec],
        out_specs=pl.BlockSpec((seq_len, hp * SB_HEAD_DIM), lambda b, j: (b, j)),
        out_shape=jax.ShapeDtypeStruct((m, d), BF16),
        scratch_shapes=[pltpu.VMEM((hp, seq_len, SB_HEAD_DIM), BF16),
                        pltpu.VMEM((hp * SB_HEAD_DIM, seq_len), BF16),
                        pltpu.VMEM((2, hp, SB_BLOCK, SB_BLOCK), F32),
                        pltpu.VMEM((hp, SB_BLOCK, SB_BLOCK), F32),
                        pltpu.VMEM((2, hp, SB_BLOCK, SB_BLOCK), BF16),
                        pltpu.VMEM((2, SUBLANES, SB_BLOCK), F32),
                        pltpu.VMEM((hp * SB_HEAD_DIM, SB_BLOCK), F32),
                        pltpu.VMEM((SUBLANES, SB_BLOCK), F32)],
        compiler_params=_cparams("parallel", "parallel"),
        name="sb_seq",
    )(q_t, k_h, v_t)


def _sb_step_kernel(pt_ref, q_ref, *refs, heads, pps):
    k_refs = refs[:pps]
    v_refs = refs[pps:2 * pps]
    o_ref, qb_ref, acc_ref, carry_ref = refs[2 * pps:]
    p = pl.program_id(1)
    d = heads * SB_HEAD_DIM
    own = (lax.broadcasted_iota(jnp.int32, (heads, d), 1) // SB_HEAD_DIM
           == lax.broadcasted_iota(jnp.int32, (heads, d), 0))

    @pl.when(p == 0)
    def _():
        q = jnp.broadcast_to(q_ref[...], (heads, d)) * (SB_HEAD_DIM ** -0.5)
        qb_ref[...] = jnp.where(own, q, 0.0).astype(BF16)
        acc_ref[...] = jnp.zeros_like(acc_ref)
        carry_ref[...] = jnp.zeros_like(carry_ref)

    row = lax.broadcasted_iota(jnp.int32, (PAGE, PAGE), 0)
    col = lax.broadcasted_iota(jnp.int32, (PAGE, PAGE), 1)
    suffix01 = jnp.where(row >= col, 1.0, 0.0).astype(BF16)
    qb = qb_ref[...]
    zs = [_dot(qb, k_refs[u][...].astype(BF16)) for u in range(pps)]
    splits = [_split3(_softplus(z)) for z in zs]
    cums = [_dot(hi, suffix01) + _dot(mid, suffix01) + _dot(lo, suffix01)
            for hi, mid, lo in splits]
    carries = [carry_ref[...]]
    for u in range(pps):
        carries.append(carries[u] + cums[u][:, 0:1])
    ws = [jnp.exp(zs[u] - cums[u] - carries[u]).astype(BF16) for u in range(pps)]
    parts = [_dot_nt(ws[u], v_refs[u][...].astype(BF16)) for u in range(pps)]
    acc = acc_ref[...]
    for part in parts:
        acc = acc + part
    acc_ref[...] = acc
    carry_ref[...] = carries[pps]

    @pl.when(p == pl.num_programs(1) - 1)
    def _():
        o_ref[...] = jnp.sum(jnp.where(own, acc, 0.0), axis=0, keepdims=True)


def _sb_step(q, page_table, cache_k, cache_v, layer):
    b, d = q.shape
    n_pages = page_table.shape[1]
    heads = d // SB_HEAD_DIM
    pps = _pick_tile(n_pages, (SB_PAGES_PER_STEP, 4, 2, 1))

    def page_spec(u):
        return pl.BlockSpec((None, None, d, PAGE),
                            lambda i, p, pt: (layer, pt[i, n_pages - 1 - (p * pps + u)], 0, 0))

    row_spec = pl.BlockSpec((None, 1, d), lambda i, p, pt: (i, 0, 0))
    page_specs = [page_spec(u) for u in range(pps)]
    grid_spec = pltpu.PrefetchScalarGridSpec(
        num_scalar_prefetch=1,
        grid=(b, n_pages // pps),
        in_specs=[row_spec] + page_specs + page_specs,
        out_specs=row_spec,
        scratch_shapes=[pltpu.VMEM((heads, d), BF16), pltpu.VMEM((heads, d), F32),
                        pltpu.VMEM((heads, 1), F32)],
    )
    out = pl.pallas_call(
        functools.partial(_sb_step_kernel, heads=heads, pps=pps),
        grid_spec=grid_spec,
        out_shape=jax.ShapeDtypeStruct((b, 1, d), F32),
        compiler_params=_cparams("parallel", "arbitrary"),
        name="sb_step",
    )(page_table, q.reshape(b, 1, d), *([cache_k] * pps), *([cache_v] * pps))
    return out.reshape(b, d)


def _gmlp_kernel(*refs, single, half, tm):
    if single:
        (x_ref, g_ref, win_ref, bin_ref, lng_ref, lnb_ref, wsv_ref, bsv_ref, wo_ref,
         o_ref, vout_ref, vn_ref, acc_ref) = refs
    else:
        (x_ref, g_ref, win_ref, bin_ref, lng_ref, lnb_ref, ws_ref, bst_ref, wo_ref,
         o_ref, vn_ref, acc_ref) = refs
    gd = half // CM_GROUPS
    pair = 2 * gd
    x = x_ref[...]
    xn = _rms(x, g_ref[...]).astype(BF16)

    v = _gelu(_dot(xn, win_ref[:, half:]) + bin_ref[:, half:])
    mu = jnp.mean(v, axis=-1, keepdims=True)
    var = jnp.mean(jnp.square(v - mu), axis=-1, keepdims=True)
    vn = (v - mu) * lax.rsqrt(var + EPS) * lng_ref[...] + lnb_ref[...]
    if single:
        vout_ref[...] = vn
        vn_ref[...] = vn
    else:
        vn_ref[...] = vn.astype(BF16)
        row = lax.broadcasted_iota(jnp.int32, (CM_CHUNK, CM_CHUNK), 0)
        col = lax.broadcasted_iota(jnp.int32, (CM_CHUNK, CM_CHUNK), 1)
        causal = col <= row

    for gp in range(CM_GROUPS // 2):
        lo = gp * pair
        u = _gelu(_dot(xn, win_ref[:, lo:lo + pair]) + bin_ref[:, lo:lo + pair])
        if single:
            s = vn_ref[:, lo:lo + pair] * wsv_ref[:, lo:lo + pair] + bsv_ref[:, lo:lo + pair]
        else:
            cols = []
            for gi in range(2):
                g = 2 * gp + gi
                ws = jnp.where(causal, ws_ref[g], 0.0).astype(BF16)
                bs = bst_ref[:, g:g + 1]
                rows = []
                for c in range(tm // CM_CHUNK):
                    vc = vn_ref[c * CM_CHUNK:(c + 1) * CM_CHUNK, g * gd:(g + 1) * gd]
                    rows.append(_dot(ws, vc) + bs)
                cols.append(jnp.concatenate(rows, axis=0) if len(rows) > 1 else rows[0])
            s = jnp.concatenate(cols, axis=1)
        y = (u * s).astype(BF16)
        contrib = _dot(y, wo_ref[lo:lo + pair, :])
        if gp == 0:
            acc_ref[...] = contrib
        else:
            acc_ref[...] += contrib
    o_ref[...] = x + acc_ref[...]


def _gmlp(r, g, w_in, b_in, ln_g, ln_b, w_s, b_s, w_o, single):
    m, d = r.shape
    half = w_o.shape[0]
    gd = half // CM_GROUPS
    tm = min(m, 128) if single else 256
    const = lambda i: (0, 0)
    specs = [
        pl.BlockSpec((tm, d), lambda i: (i, 0)),
        pl.BlockSpec((1, d), const),
        pl.BlockSpec((d, 2 * half), const),
        pl.BlockSpec((1, 2 * half), const),
        pl.BlockSpec((1, half), const),
        pl.BlockSpec((1, half), const),
    ]
    args = [r, g.reshape(1, d), w_in, b_in.reshape(1, -1), ln_g.reshape(1, -1), ln_b.reshape(1, -1)]
    if single:
        args += [jnp.repeat(w_s[:, 0, 0], gd).reshape(1, half), jnp.repeat(b_s[:, 0], gd).reshape(1, half)]
        specs += [pl.BlockSpec((1, half), const), pl.BlockSpec((1, half), const)]
    else:
        args += [w_s, jnp.transpose(b_s)]
        specs += [pl.BlockSpec((CM_GROUPS, CM_CHUNK, CM_CHUNK), lambda i: (0, 0, 0)),
                  pl.BlockSpec((CM_CHUNK, CM_GROUPS), const)]
    args.append(w_o)
    specs.append(pl.BlockSpec((half, d), const))
    row_out = pl.BlockSpec((tm, d), lambda i: (i, 0))
    if single:
        out_specs = [row_out, pl.BlockSpec((tm, half), lambda i: (i, 0))]
        out_shape = [jax.ShapeDtypeStruct((m, d), F32), jax.ShapeDtypeStruct((m, half), F32)]
        scratch = [pltpu.VMEM((tm, half), F32), pltpu.VMEM((tm, d), F32)]
    else:
        out_specs = row_out
        out_shape = jax.ShapeDtypeStruct((m, d), F32)
        scratch = [pltpu.VMEM((tm, half), BF16), pltpu.VMEM((tm, d), F32)]
    return pl.pallas_call(
        functools.partial(_gmlp_kernel, single=single, half=half, tm=tm),
        grid=(m // tm,),
        in_specs=specs,
        out_specs=out_specs,
        out_shape=out_shape,
        scratch_shapes=scratch,
        compiler_params=_cparams("parallel"),
        name="gmlp_step" if single else "gmlp_seq",
    )(*args)


def _ml_qkv_tail(xc, x, wq_ref, wk_ref, wv_ref, q_ref, k_ref, v_ref, scale):
    xc = xc.astype(BF16)
    q_ref[...] = _dot(xc, wq_ref[...]).astype(q_ref.dtype)
    k_ref[...] = (_dot(xc, wk_ref[...]) * scale).astype(k_ref.dtype)
    v_ref[...] = _dot(x.astype(BF16), wv_ref[...]).astype(v_ref.dtype)


def _ml_qkv_seq_kernel(x_ref, halo_ref, cw_ref, cb_ref, wq_ref, wk_ref, wv_ref, q_ref, k_ref, v_ref, xs_ref,
                       *, tiles_per_seq, tm, scale):
    i = pl.program_id(1)
    x = x_ref[...]
    xs_ref[:ML_HALO, :] = jnp.where(i % tiles_per_seq == 0, 0.0, halo_ref[...])
    xs_ref[ML_HALO:, :] = x
    y = cb_ref[...]
    for t in range(ML_CONV - 1):
        off = ML_HALO - (ML_CONV - 1) + t
        y = y + cw_ref[t:t + 1, :] * xs_ref[off:off + tm, :]
    y = y + cw_ref[ML_CONV - 1:ML_CONV, :] * x
    _ml_qkv_tail(_silu(y), x, wq_ref, wk_ref, wv_ref, q_ref, k_ref, v_ref, scale)


def _ml_qkv_step_kernel(x_ref, b0_ref, b1_ref, b2_ref, cw_ref, cb_ref, wq_ref, wk_ref, wv_ref,
                        q_ref, k_ref, v_ref, *, scale):
    x = x_ref[...]
    y = (cb_ref[...] + cw_ref[0:1, :] * b0_ref[...] + cw_ref[1:2, :] * b1_ref[...]
         + cw_ref[2:3, :] * b2_ref[...] + cw_ref[3:4, :] * x)
    _ml_qkv_tail(_silu(y), x, wq_ref, wk_ref, wv_ref, q_ref, k_ref, v_ref, scale)


def _ml_qkv(proj, conv_w, conv_b, w_q, w_k, w_v, *, seq_len=None, bufs=None, qv_dtype=BF16):
    m = proj.shape[0]
    nh, hd, _ = w_q.shape
    inner = nh * hd
    scale = hd ** -0.5
    cb = conv_b.reshape(1, inner)
    wspec = pl.BlockSpec((None, hd, hd), lambda h, i: (h, 0, 0))
    cwspec = pl.BlockSpec((ML_CONV, hd), lambda h, i: (0, h))
    cbspec = pl.BlockSpec((1, hd), lambda h, i: (0, h))
    if bufs is None:
        tm = _pick_tile(seq_len, (512, 256, 128))
        hb = tm // ML_HALO
        xspec = pl.BlockSpec((tm, hd), lambda h, i: (i, h))
        kern = functools.partial(_ml_qkv_seq_kernel, tiles_per_seq=seq_len // tm, tm=tm, scale=scale)
        args = [proj, proj, conv_w, cb, w_q, w_k, w_v]
        specs = [xspec, pl.BlockSpec((ML_HALO, hd), lambda h, i: (jnp.maximum(i * hb - 1, 0), h)),
                 cwspec, cbspec, wspec, wspec, wspec]
        scratch = [pltpu.VMEM((tm + ML_HALO, hd), F32)]
    else:
        tm = m
        xspec = pl.BlockSpec((tm, hd), lambda h, i: (i, h))
        kern = functools.partial(_ml_qkv_step_kernel, scale=scale)
        args = [proj, *bufs, conv_w, cb, w_q, w_k, w_v]
        specs = [xspec, xspec, xspec, xspec, cwspec, cbspec, wspec, wspec, wspec]
        scratch = []
    return pl.pallas_call(
        kern,
        grid=(nh, m // tm),
        in_specs=specs,
        out_specs=[xspec, xspec, xspec],
        out_shape=[jax.ShapeDtypeStruct((m, inner), qv_dtype), jax.ShapeDtypeStruct((m, inner), F32),
                   jax.ShapeDtypeStruct((m, inner), qv_dtype)],
        scratch_shapes=scratch,
        compiler_params=_cparams("parallel", "parallel"),
        name="ml_qkv",
    )(*args)


def _lane_select(x, idx):
    lane = lax.broadcasted_iota(jnp.int32, x.shape, 1)
    return jnp.sum(jnp.where(lane == idx, x, 0.0), axis=1, keepdims=True)


def _ml_chunk_kernel(q_ref, k_ref, v_ref, gt_ref, h_ref, c_out, n_out, m_out, c_ref, n_ref, m_ref, *, heads):
    hidx = pl.program_id(1)
    c = pl.program_id(2)
    L = ML_CHUNK

    @pl.when(c == 0)
    def _():
        c_ref[...] = jnp.zeros_like(c_ref)
        n_ref[...] = jnp.zeros_like(n_ref)
        m_ref[...] = jnp.zeros_like(m_ref)

    gt = gt_ref[...]
    i_col = _lane_select(gt, hidx)
    f_col = _log_sigmoid(_lane_select(gt, hidx + heads))
    row = lax.broadcasted_iota(jnp.int32, (L, L), 0)
    col = lax.broadcasted_iota(jnp.int32, (L, L), 1)
    causal = col <= row
    lower01 = jnp.where(causal, 1.0, 0.0).astype(BF16)
    ones01 = jnp.ones((L, L), BF16)
    f_b = jnp.broadcast_to(f_col, (L, L))
    i_b = jnp.broadcast_to(i_col, (L, L))
    b_c = _dot_exact_lhs(lower01, f_b)
    b_r = _dot_exact_lhs(ones01, jnp.where(row <= col, f_b, 0.0))
    i_r = _dot_exact_lhs(ones01, jnp.where(row == col, i_b, 0.0))
    b_col = b_c[:, 0:1]
    m_prev = m_ref[0:1, 0:1]

    d_log = jnp.where(causal, b_c - b_r + i_r, -jnp.inf)
    inter = b_col + m_prev
    m_t = jnp.maximum(inter, jnp.max(d_log, axis=1, keepdims=True))
    w_intra = jnp.exp(d_log - m_t)
    w_inter = jnp.exp(inter - m_t)

    q = q_ref[...].astype(BF16)
    k = k_ref[...]
    kb = k.astype(BF16)
    v = v_ref[...].astype(BF16)
    c_old = c_ref[...]
    n_old = n_ref[...]
    qk = _dot_nt(q, kb) * w_intra
    num = w_inter * _dot(q, c_old.astype(BF16)) + _dot(qk.astype(BF16), v)
    qn = _dot_nt(q, n_old.astype(BF16))[:, 0:1]
    den = w_inter * qn + jnp.sum(qk, axis=1, keepdims=True)
    h_ref[...] = num / jnp.maximum(jnp.abs(den), jnp.exp(-m_t))

    m_new = m_t[L - 1:L, :]
    b_last = b_col[L - 1:L, :]
    w_state = jnp.exp(b_last + m_prev - m_new)
    w_rows_c = jnp.exp(b_last - b_col + i_col - m_new)
    w_rows_r = jnp.exp(b_last - b_r[0:SUBLANES, :] + i_r[0:SUBLANES, :] - m_new)
    kw = (k * w_rows_c).astype(BF16)
    c_ref[...] = w_state * c_old + _dot_tn(kw, v)
    n_ref[...] = w_state * n_old + _dot(w_rows_r.astype(BF16), kb)
    m_ref[...] = jnp.broadcast_to(m_new, m_ref.shape)

    @pl.when(c == pl.num_programs(2) - 1)
    def _():
        c_out[...] = c_ref[...]
        n_out[...] = n_ref[0:1, :]
        m_out[...] = m_ref[0:1, :]


def _ml_chunks(q, k, v, gates, batch, seq_len, heads):
    m, inner = q.shape
    hd = inner // heads
    nc = seq_len // ML_CHUNK
    xspec = pl.BlockSpec((ML_CHUNK, hd), lambda b, h, c: (b * nc + c, h))
    return pl.pallas_call(
        functools.partial(_ml_chunk_kernel, heads=heads),
        grid=(batch, heads, nc),
        in_specs=[xspec, xspec, xspec, pl.BlockSpec((ML_CHUNK, LANES), lambda b, h, c: (b * nc + c, 0))],
        out_specs=[xspec,
                   pl.BlockSpec((None, None, hd, hd), lambda b, h, c: (b, h, 0, 0)),
                   pl.BlockSpec((None, None, 1, hd), lambda b, h, c: (b, h, 0, 0)),
                   pl.BlockSpec((None, None, 1, LANES), lambda b, h, c: (b, h, 0, 0))],
        out_shape=[jax.ShapeDtypeStruct((m, inner), F32),
                   jax.ShapeDtypeStruct((batch, heads, hd, hd), F32),
                   jax.ShapeDtypeStruct((batch, heads, 1, hd), F32),
                   jax.ShapeDtypeStruct((batch, heads, 1, LANES), F32)],
        scratch_shapes=[pltpu.VMEM((hd, hd), F32), pltpu.VMEM((SUBLANES, hd), F32),
                        pltpu.VMEM((SUBLANES, LANES), F32)],
        compiler_params=_cparams("parallel", "parallel", "arbitrary"),
        name="ml_chunks",
    )(q, k, v, gates)


def _ml_step_kernel(q_ref, k_ref, v_ref, gt_ref, mp_ref, c_ref, n_ref, h_ref, c_out, n_out, m_out, *, heads):
    gt = gt_ref[...]
    mp = mp_ref[...]
    hd = q_ref.shape[1] // heads
    lane = lax.broadcasted_iota(jnp.int32, (1, LANES), 1)
    first_row = lax.broadcasted_iota(jnp.int32, (SUBLANES, hd), 0) == 0
    m_all = jnp.zeros((1, LANES), F32)
    for j in range(heads):
        cols = slice(j * hd, (j + 1) * hd)
        i_g = _lane_select(gt, j)
        f_g = _log_sigmoid(_lane_select(gt, j + heads))
        inter = f_g + _lane_select(mp, j)
        m_t = jnp.maximum(inter, i_g)
        w_in = jnp.exp(i_g - m_t)
        w_st = jnp.exp(inter - m_t)

        q = q_ref[:, cols]
        k = k_ref[:, cols]
        v = v_ref[:, cols]
        n = n_ref[:, cols]
        c_old = c_ref[j]
        q8 = jnp.broadcast_to(q, (SUBLANES, hd)).astype(BF16)
        q_c = _dot(q8, c_old.astype(BF16))[0:1, :]
        qk = jnp.sum(q * k, axis=1, keepdims=True) * w_in
        num = w_st * q_c + qk * v
        den = w_st * jnp.sum(q * n, axis=1, keepdims=True) + qk
        h_ref[:, cols] = num / jnp.maximum(jnp.abs(den), jnp.exp(-m_t))

        kw = k * w_in
        kw8 = jnp.where(first_row, jnp.broadcast_to(kw, (SUBLANES, hd)), 0.0).astype(BF16)
        v8 = jnp.broadcast_to(v, (SUBLANES, hd)).astype(BF16)
        c_out[j] = w_st * c_old + _dot_tn(kw8, v8)
        n_out[:, cols] = w_st * n + kw
        m_all = jnp.where(lane == j, m_t, m_all)
    m_out[...] = m_all


def _ml_step(q, k, v, gates, m_prev, c_state, n_state, heads):
    b, inner = q.shape
    hd = inner // heads
    r3 = lambda a: a.reshape(b, 1, -1)
    vspec = pl.BlockSpec((None, 1, inner), lambda i: (i, 0, 0))
    gspec = pl.BlockSpec((None, 1, LANES), lambda i: (i, 0, 0))
    cspec = pl.BlockSpec((None, heads, hd, hd), lambda i: (i, 0, 0, 0))
    return pl.pallas_call(
        functools.partial(_ml_step_kernel, heads=heads),
        grid=(b,),
        in_specs=[vspec, vspec, vspec, gspec, gspec, cspec, vspec],
        out_specs=[vspec, cspec, vspec, gspec],
        out_shape=[jax.ShapeDtypeStruct((b, 1, inner), F32),
                   jax.ShapeDtypeStruct((b, heads, hd, hd), F32),
                   jax.ShapeDtypeStruct((b, 1, inner), F32),
                   jax.ShapeDtypeStruct((b, 1, LANES), F32)],
        compiler_params=_cparams("parallel"),
        name="ml_step",
    )(r3(q), r3(k), r3(v), r3(gates), r3(m_prev), c_state, r3(n_state))


def _ml_out_kernel(h_ref, o_ref, ng_ref, wo_ref, r_ref, out_ref, *, heads):
    h = h_ref[...]
    hd = h.shape[1] // heads
    parts = []
    for j in range(heads):
        hh = h[:, j * hd:(j + 1) * hd]
        mu = jnp.mean(hh, axis=-1, keepdims=True)
        var = jnp.mean(jnp.square(hh - mu), axis=-1, keepdims=True)
        parts.append((hh - mu) * lax.rsqrt(var + EPS))
    hn = jnp.concatenate(parts, axis=1) * ng_ref[...]
    out = (_sigmoid(o_ref[...]) * hn).astype(BF16)
    out_ref[...] = r_ref[...] + _dot(out, wo_ref[...])


def _ml_out(h, proj, norm_g, w_o, r, heads):
    m, inner = h.shape
    d = r.shape[1]
    tm = _pick_tile(m, (256, 128))
    return pl.pallas_call(
        functools.partial(_ml_out_kernel, heads=heads),
        grid=(m // tm,),
        in_specs=[pl.BlockSpec((tm, inner), lambda i: (i, 0)),
                  pl.BlockSpec((tm, inner), lambda i: (i, 1)),
                  pl.BlockSpec((1, inner), lambda i: (0, 0)),
                  pl.BlockSpec((inner, d), lambda i: (0, 0)),
                  pl.BlockSpec((tm, d), lambda i: (i, 0))],
        out_specs=pl.BlockSpec((tm, d), lambda i: (i, 0)),
        out_shape=jax.ShapeDtypeStruct((m, d), F32),
        compiler_params=_cparams("parallel"),
        name="ml_out",
    )(h, proj, norm_g.reshape(1, inner), w_o, r)


def _trunk(x, p, seq_len, weights, *, past=None, ml_state=None, ffn_state=None):
    (norm_mix, norm_ffn, norm_ple, norm_final, sb_w_qkv, sb_w_o, cm_w_in, cm_b_in, cm_ln_g, cm_ln_b,
     cm_w_s, cm_b_s, cm_w_o, ml_w_in, ml_b_gates, ml_conv_w, ml_conv_b, ml_w_q, ml_w_k, ml_w_v,
     ml_norm_g, ml_w_o, ffn_w_up, ffn_conv_w, ffn_conv_b, ffn_w_down, ple_w_proj, ple_w_gate) = weights
    m, d = x.shape
    batch = m // seq_len
    depth = norm_mix.shape[0]
    n_sb = sb_w_qkv.shape[0]
    step = seq_len == 1
    heads = d // SB_HEAD_DIM
    outs = dict(k=[], v=[], cm=[], c=[], n=[], m=[], conv=[], ffn=[])
    kt_all = vt_all = None
    r = x
    for i in range(depth):
        kind, j = i % 3, i // 3
        if kind == 0:
            if step:
                qkv = _matmul(r, sb_w_qkv[j], g=norm_mix[i], name="sb_qkv")
                q, k, v = qkv[:, :d], qkv[:, d:2 * d], qkv[:, 2 * d:]
                outs["k"].append(k.reshape(batch, seq_len, heads, SB_HEAD_DIM))
                outs["v"].append(v.reshape(batch, seq_len, heads, SB_HEAD_DIM))
                page_table, cache_k, cache_v = past
                o = _sb_step(q, page_table, cache_k, cache_v, j)
            else:
                kt_all, vt_all, q_t, v_t, k_h = _sb_qkv_seq(r, norm_mix[i], sb_w_qkv[j], j, n_sb, kt_all, vt_all,
                                                           seq_len)
                o = _sb_seq(q_t, k_h, v_t, batch, seq_len)
            r = _matmul(o, sb_w_o[j], res=r, name="sb_out")
        elif kind == 1:
            res = _gmlp(r, norm_mix[i], cm_w_in[j], cm_b_in[j], cm_ln_g[j], cm_ln_b[j], cm_w_s[j], cm_b_s[j],
                        cm_w_o[j], step)
            if step:
                r, vn = res
                outs["cm"].append(vn.reshape(batch, 1, -1))
            else:
                r = res
        else:
            nh = ML_HEADS
            inner = ml_w_q.shape[2] * nh
            w_main = ml_w_in[j][:, :2 * inner]
            w_gate = jnp.pad(ml_w_in[j][:, 2 * inner:], ((0, 0), (0, LANES - 2 * nh)))
            b_gate = jnp.pad(ml_b_gates[j], (0, LANES - 2 * nh))
            proj = _matmul(r, w_main, g=norm_mix[i], name="ml_in")
            gates = _matmul(r, w_gate, g=norm_mix[i], bias=b_gate, name="ml_gates")
            if step:
                c0, n0, m0, conv0 = ml_state
                bufs = [conv0[j][:, t] for t in range(ML_CONV - 1)]
                q, k, v = _ml_qkv(proj, ml_conv_w[j], ml_conv_b[j], ml_w_q[j], ml_w_k[j], ml_w_v[j],
                                  bufs=bufs, qv_dtype=F32)
                m_prev = jnp.pad(m0[j], ((0, 0), (0, LANES - nh)))
                hh, c_new, n_new, m_new = _ml_step(q, k, v, gates, m_prev, c0[j], n0[j], nh)
                hh = hh.reshape(m, inner)
                m_new = m_new[:, 0, :nh]
                outs["conv"].append(jnp.concatenate([conv0[j][:, 1:], proj[:, None, :inner]], axis=1))
            else:
                q, k, v = _ml_qkv(proj, ml_conv_w[j], ml_conv_b[j], ml_w_q[j], ml_w_k[j], ml_w_v[j],
                                  seq_len=seq_len)
                hh, c_new, n_new, m_new = _ml_chunks(q, k, v, gates, batch, seq_len, nh)
                m_new = m_new[:, :, 0, 0]
                tail = proj.reshape(batch, seq_len, -1)[:, seq_len - (ML_CONV - 1):, :inner]
                outs["conv"].append(tail)
            outs["c"].append(c_new)
            outs["n"].append(n_new.reshape(batch, nh, -1))
            outs["m"].append(m_new)
            r = _ml_out(hh, proj, ml_norm_g[j], ml_w_o[j], r, nh)

        if step:
            a = _matmul(r, ffn_w_up[i], g=norm_ffn[i], name="ffn_up")
            buf = ffn_state[i]
            r = _ffn_step(r, a, buf[:, 0], buf[:, 1], ffn_conv_w[i], ffn_conv_b[i], ffn_w_down[i])
            outs["ffn"].append(jnp.concatenate([buf[:, 1:], a[:, None, :]], axis=1))
        else:
            tail = r.reshape(batch, seq_len, d)[:, seq_len - (FFN_CONV - 1):].reshape(-1, d)
            a_tail = _matmul(tail, ffn_w_up[i], g=norm_ffn[i], name="ffn_tail")
            outs["ffn"].append(a_tail.reshape(batch, FFN_CONV - 1, -1))
            r = _ffn_seq(r, norm_ffn[i], ffn_w_up[i], ffn_conv_w[i], ffn_conv_b[i], ffn_w_down[i], seq_len)

        r = _ple(r, norm_ple[i], ple_w_gate[i], p[i], ple_w_proj[i],
                 g_final=norm_final if i == depth - 1 else None)
    if not step:
        rows = lambda a: a.reshape(n_sb, batch, heads, SB_HEAD_DIM, seq_len).transpose(0, 1, 4, 2, 3)
        outs["k"], outs["v"] = rows(kt_all), rows(vt_all)
    else:
        outs["k"], outs["v"] = jnp.stack(outs["k"]), jnp.stack(outs["v"])
    return r, outs


def kernel(x_prompt, x_sample, p_prompt, p_sample, page_table, cache_k, cache_v, state_mlstm_c, state_mlstm_n, state_mlstm_m, state_mlstm_conv, state_ffn_conv, norm_mix, norm_ffn, norm_ple, norm_final, sb_w_qkv, sb_w_o, cm_w_in, cm_b_in, cm_ln_g, cm_ln_b, cm_w_s, cm_b_s, cm_w_o, ml_w_in, ml_b_gates, ml_conv_w, ml_conv_b, ml_w_q, ml_w_k, ml_w_v, ml_norm_g, ml_w_o, ffn_w_up, ffn_conv_w, ffn_conv_b, ffn_w_down, ple_w_proj, ple_w_gate):
    bf = lambda w: w.astype(BF16)
    weights = (norm_mix, norm_ffn, norm_ple, norm_final, bf(sb_w_qkv), bf(sb_w_o), bf(cm_w_in), cm_b_in,
               cm_ln_g, cm_ln_b, cm_w_s, cm_b_s, bf(cm_w_o), bf(ml_w_in), ml_b_gates, ml_conv_w, ml_conv_b,
               bf(ml_w_q), bf(ml_w_k), bf(ml_w_v), ml_norm_g, bf(ml_w_o), bf(ffn_w_up), ffn_conv_w,
               ffn_conv_b, bf(ffn_w_down), bf(ple_w_proj), bf(ple_w_gate))
    b, t, d = x_prompt.shape
    bs, ts, _ = x_sample.shape
    depth = p_prompt.shape[0]
    n_sb, n_phys, page, sbh, sbd = cache_k.shape
    pages = lambda c: c.transpose(0, 1, 3, 4, 2).reshape(n_sb, n_phys, sbh * sbd, page)

    y_p, o_p = _trunk(x_prompt.reshape(b * t, d), p_prompt.reshape(depth, b * t, -1), t, weights)
    y_s, o_s = _trunk(
        x_sample.reshape(bs * ts, d), p_sample.reshape(depth, bs * ts, -1), ts, weights,
        past=(page_table, pages(cache_k), pages(cache_v)),
        ml_state=(state_mlstm_c, state_mlstm_n, state_mlstm_m, state_mlstm_conv),
        ffn_state=state_ffn_conv)

    st = jnp.stack
    return (y_p.reshape(b, t, d), y_s.reshape(bs, ts, d),
            o_p["k"], o_p["v"], o_s["k"], o_s["v"], st(o_s["cm"]),
            st(o_p["c"]), st(o_p["n"]), st(o_p["m"]), st(o_p["conv"]),
            st(o_s["c"]), st(o_s["n"]), st(o_s["m"]), st(o_s["conv"]),
            st(o_p["ffn"]), st(o_s["ffn"]))
```

```python
import functools

import jax
import jax.numpy as jnp
from jax import lax
from jax.experimental import pallas as pl
from jax.experimental.pallas import tpu as pltpu

F32 = jnp.float32
BF16 = jnp.bfloat16
EPS = 1e-6

VMEM_LIMIT_BYTES = 56 * 1024 * 1024
LANES = 128
SUBLANES = 8

SB_HEAD_DIM = 64
SB_BLOCK = 256
SB_SEG = SB_BLOCK // SUBLANES
SB_HEADS_PER_STEP = 4
SB_PAGES_PER_STEP = 8
PAGE = 128
CM_GROUPS = 8
CM_CHUNK = 128
ML_HEADS = 4
ML_CHUNK = 128
ML_CONV = 4
FFN_CONV = 3
FFN_HALO = 16
ML_HALO = 8


def _cparams(*sem):
    return pltpu.CompilerParams(dimension_semantics=sem, vmem_limit_bytes=VMEM_LIMIT_BYTES)


def _dot(a, b):
    return jnp.dot(a, b, preferred_element_type=F32)


def _dot_nt(a, b):
    return lax.dot_general(a, b, (((1,), (1,)), ((), ())), preferred_element_type=F32)


def _dot_tn(a, b):
    return lax.dot_general(a, b, (((0,), (0,)), ((), ())), preferred_element_type=F32)


def _rms(x, g):
    ms = jnp.mean(x * x, axis=-1, keepdims=True)
    return x * lax.rsqrt(ms + EPS) * g


def _sigmoid(x):
    return 1.0 / (1.0 + jnp.exp(-x))


def _silu(x):
    return x * _sigmoid(x)


def _gelu(x):
    return 0.5 * x * (1.0 + jnp.tanh(0.7978845608028654 * (x + 0.044715 * (x * x * x))))


def _softplus(x):
    return jnp.maximum(x, 0.0) + jnp.log(1.0 + jnp.exp(-jnp.abs(x)))


def _log_sigmoid(x):
    return jnp.minimum(x, 0.0) - jnp.log(1.0 + jnp.exp(-jnp.abs(x)))


def _split3(x):
    hi = x.astype(BF16)
    r1 = x - hi.astype(F32)
    mid = r1.astype(BF16)
    lo = (r1 - mid.astype(F32)).astype(BF16)
    return hi, mid, lo


def _dot_exact_lhs(lhs01, x):
    hi, mid, lo = _split3(x)
    return _dot(lhs01, hi) + _dot(lhs01, mid) + _dot(lhs01, lo)


def _dot_exact_rhs(x, rhs01):
    hi, mid, lo = _split3(x)
    return _dot(hi, rhs01) + _dot(mid, rhs01) + _dot(lo, rhs01)


def _pick_tile(n, candidates):
    for c in candidates:
        if n % c == 0:
            return c
    return n


def _mm_kernel(*refs, norm, act, has_bias, has_res):
    it = iter(refs)
    x_ref = next(it)
    g_ref = next(it) if norm else None
    w_ref = next(it)
    b_ref = next(it) if has_bias else None
    r_ref = next(it) if has_res else None
    o_ref = next(it)
    xn_ref = next(it)

    @pl.when(pl.program_id(1) == 0)
    def _():
        x = x_ref[...].astype(F32)
        if norm:
            x = _rms(x, g_ref[...])
        xn_ref[...] = x.astype(BF16)

    acc = _dot(xn_ref[...], w_ref[...])
    if has_bias:
        acc = acc + b_ref[...]
    if act == "gelu":
        acc = _gelu(acc)
    if has_res:
        acc = acc + r_ref[...]
    o_ref[...] = acc.astype(o_ref.dtype)


def _matmul(x, w, *, g=None, bias=None, res=None, act=None, out_dtype=F32, name="matmul"):
    m, k = x.shape
    n = w.shape[1]
    tm = _pick_tile(m, (512, 256, 128, 16, 8))
    tn = _pick_tile(n, (512, 384, 256, 128))
    norm = g is not None
    args = [x]
    specs = [pl.BlockSpec((tm, k), lambda i, j: (i, 0))]
    if norm:
        args.append(g.reshape(1, k))
        specs.append(pl.BlockSpec((1, k), lambda i, j: (0, 0)))
    args.append(w)
    specs.append(pl.BlockSpec((k, tn), lambda i, j: (0, j)))
    if bias is not None:
        args.append(bias.reshape(1, n))
        specs.append(pl.BlockSpec((1, tn), lambda i, j: (0, j)))
    if res is not None:
        args.append(res)
        specs.append(pl.BlockSpec((tm, tn), lambda i, j: (i, j)))
    kern = functools.partial(_mm_kernel, norm=norm, act=act, has_bias=bias is not None,
                             has_res=res is not None)
    return pl.pallas_call(
        kern,
        grid=(m // tm, n // tn),
        in_specs=specs,
        out_specs=pl.BlockSpec((tm, tn), lambda i, j: (i, j)),
        out_shape=jax.ShapeDtypeStruct((m, n), out_dtype),
        scratch_shapes=[pltpu.VMEM((tm, k), BF16)],
        compiler_params=_cparams("parallel", "arbitrary"),
        name=name,
    )(*args)


def _mm_rows_kernel(*refs, norm, has_res):
    it = iter(refs)
    x_ref = next(it)
    g_ref = next(it) if norm else None
    w_ref = next(it)
    r_ref = next(it) if has_res else None
    o_ref = next(it)
    x = x_ref[...].astype(F32)
    if norm:
        x = _rms(x, g_ref[...])
    acc = _dot(x.astype(BF16), w_ref[...])
    if has_res:
        acc = acc + r_ref[...]
    o_ref[...] = acc.astype(o_ref.dtype)


def _matmul_rows(x, w, *, tm, g=None, res=None, out_dtype=F32, name="matmul_rows"):
    m, k = x.shape
    n = w.shape[1]
    norm = g is not None
    args = [x]
    specs = [pl.BlockSpec((tm, k), lambda i: (i, 0))]
    if norm:
        args.append(g.reshape(1, k))
        specs.append(pl.BlockSpec((1, k), lambda i: (0, 0)))
    args.append(w)
    specs.append(pl.BlockSpec((k, n), lambda i: (0, 0)))
    if res is not None:
        args.append(res)
        specs.append(pl.BlockSpec((tm, n), lambda i: (i, 0)))
    return pl.pallas_call(
        functools.partial(_mm_rows_kernel, norm=norm, has_res=res is not None),
        grid=(m // tm,),
        in_specs=specs,
        out_specs=pl.BlockSpec((tm, n), lambda i: (i, 0)),
        out_shape=jax.ShapeDtypeStruct((m, n), out_dtype),
        compiler_params=_cparams("parallel"),
        name=name,
    )(*args)


def _ffn_seq_kernel(*refs, tiles_per_seq, tm, tf, final):
    (x_ref, halo_ref, g_ref, wg_ref, wu_ref, cwg_ref, cwu_ref, cbg_ref, cbu_ref, wd_ref,
     gp_ref, wgate_ref, p_ref, wproj_ref) = refs[:14]
    gf_ref = refs[14] if final else None
    o_ref, xn_ref, a_ref, c_ref = refs[15 if final else 14:]
    i = pl.program_id(0)
    j = pl.program_id(1)

    @pl.when(j == 0)
    def _():
        g = g_ref[...]
        xn_ref[FFN_HALO:, :] = _rms(x_ref[...], g).astype(BF16)
        hal = _rms(halo_ref[...], g)
        hal = jnp.where(i % tiles_per_seq == 0, 0.0, hal)
        xn_ref[:FFN_HALO, :] = hal.astype(BF16)

    xa = xn_ref[...]
    a_ref[:, :tf] = _dot(xa, wg_ref[...])
    a_ref[:, tf:] = _dot(xa, wu_ref[...])

    def conv(lo, cw_ref, cb_ref):
        y = cb_ref[...]
        for t in range(FFN_CONV):
            off = FFN_HALO - (FFN_CONV - 1) + t
            y = y + cw_ref[t:t + 1, :] * a_ref[off:off + tm, lo:lo + tf]
        return y

    cg = conv(0, cwg_ref, cbg_ref)
    cu = conv(tf, cwu_ref, cbu_ref)
    c_ref[:, pl.ds(pl.multiple_of(j * tf, tf), tf)] = (_silu(cg) * cu).astype(BF16)

    @pl.when(j == pl.num_programs(1) - 1)
    def _():
        r = x_ref[...] + _dot(c_ref[...], wd_ref[...])
        gate = _sigmoid(_dot(_rms(r, gp_ref[...]).astype(BF16), wgate_ref[...]))
        out = r + gate * _dot(p_ref[...].astype(BF16), wproj_ref[...])
        if final:
            out = _rms(out, gf_ref[...])
        o_ref[...] = out


def _ffn_ple_seq(r, g, w_up, conv_w, conv_b, w_down, seq_len, g_ple, w_gate, p, w_proj, g_final=None):
    m, d = r.shape
    dff = w_down.shape[0]
    pd = p.shape[1]
    tm = _pick_tile(seq_len, (1024, 512, 256, 128))
    tf = 256
    nj = dff // tf
    hb = tm // FFN_HALO
    final = g_final is not None
    const = lambda i, j: (0, 0)
    once = pl.Buffered(1)
    args = [r, r, g.reshape(1, d), w_up, w_up, conv_w, conv_w, conv_b.reshape(1, -1), conv_b.reshape(1, -1),
            w_down, g_ple.reshape(1, d), w_gate, p, w_proj]
    specs = [
        pl.BlockSpec((tm, d), lambda i, j: (i, 0)),
        pl.BlockSpec((FFN_HALO, d), lambda i, j: (jnp.maximum(i * hb - 1, 0), 0)),
        pl.BlockSpec((1, d), const),
        pl.BlockSpec((d, tf), lambda i, j: (0, j)),
        pl.BlockSpec((d, tf), lambda i, j: (0, nj + j)),
        pl.BlockSpec((FFN_CONV, tf), lambda i, j: (0, j)),
        pl.BlockSpec((FFN_CONV, tf), lambda i, j: (0, nj + j)),
        pl.BlockSpec((1, tf), lambda i, j: (0, j)),
        pl.BlockSpec((1, tf), lambda i, j: (0, nj + j)),
        pl.BlockSpec((dff, d), const, pipeline_mode=once),
        pl.BlockSpec((1, d), const),
        pl.BlockSpec((d, d), const, pipeline_mode=once),
        pl.BlockSpec((tm, pd), lambda i, j: (i, 0)),
        pl.BlockSpec((pd, d), const, pipeline_mode=once),
    ]
    if final:
        args.append(g_final.reshape(1, d))
        specs.append(pl.BlockSpec((1, d), const))
    kern = functools.partial(_ffn_seq_kernel, tiles_per_seq=seq_len // tm, tm=tm, tf=tf, final=final)
    return pl.pallas_call(
        kern,
        grid=(m // tm, nj),
        in_specs=specs,
        out_specs=pl.BlockSpec((tm, d), lambda i, j: (i, 0)),
        out_shape=jax.ShapeDtypeStruct((m, d), F32),
        scratch_shapes=[pltpu.VMEM((tm + FFN_HALO, d), BF16),
                        pltpu.VMEM((tm + FFN_HALO, 2 * tf), F32),
                        pltpu.VMEM((tm, dff), BF16)],
        compiler_params=_cparams("parallel", "arbitrary"),
        name="ffn_ple_seq",
    )(*args)


def _ffn_step_kernel(ag_ref, au_ref, b0g_ref, b0u_ref, b1g_ref, b1u_ref, cwg_ref, cwu_ref, cbg_ref, cbu_ref,
                     wd_ref, r_ref, o_ref, acc_ref):
    j = pl.program_id(0)

    @pl.when(j == 0)
    def _():
        acc_ref[...] = jnp.zeros_like(acc_ref)

    def conv(a_ref, b0_ref, b1_ref, cw_ref, cb_ref):
        return (cb_ref[...] + cw_ref[0:1, :] * b0_ref[...] + cw_ref[1:2, :] * b1_ref[...]
                + cw_ref[2:3, :] * a_ref[...])

    cg = conv(ag_ref, b0g_ref, b1g_ref, cwg_ref, cbg_ref)
    cu = conv(au_ref, b0u_ref, b1u_ref, cwu_ref, cbu_ref)
    acc_ref[...] += _dot((_silu(cg) * cu).astype(BF16), wd_ref[...])

    @pl.when(j == pl.num_programs(0) - 1)
    def _():
        o_ref[...] = r_ref[...] + acc_ref[...]


def _ffn_step(r, a, buf0, buf1, conv_w, conv_b, w_down):
    m, d = r.shape
    dff = w_down.shape[0]
    tf = 256
    nj = dff // tf
    lo = lambda j: (0, j)
    hi = lambda j: (0, nj + j)
    cb = conv_b.reshape(1, -1)
    return pl.pallas_call(
        _ffn_step_kernel,
        grid=(nj,),
        in_specs=[
            pl.BlockSpec((m, tf), lo), pl.BlockSpec((m, tf), hi),
            pl.BlockSpec((m, tf), lo), pl.BlockSpec((m, tf), hi),
            pl.BlockSpec((m, tf), lo), pl.BlockSpec((m, tf), hi),
            pl.BlockSpec((FFN_CONV, tf), lo), pl.BlockSpec((FFN_CONV, tf), hi),
            pl.BlockSpec((1, tf), lo), pl.BlockSpec((1, tf), hi),
            pl.BlockSpec((tf, d), lambda j: (j, 0)),
            pl.BlockSpec((m, d), lambda j: (0, 0)),
        ],
        out_specs=pl.BlockSpec((m, d), lambda j: (0, 0)),
        out_shape=jax.ShapeDtypeStruct((m, d), F32),
        scratch_shapes=[pltpu.VMEM((m, d), F32)],
        compiler_params=_cparams("arbitrary"),
        name="ffn_step",
    )(a, a, buf0, buf0, buf1, buf1, conv_w, conv_w, cb, cb, w_down, r)


def _ple_kernel(*refs, final):
    if final:
        r_ref, g_ref, wg_ref, p_ref, wp_ref, gf_ref, o_ref = refs
    else:
        r_ref, g_ref, wg_ref, p_ref, wp_ref, o_ref = refs
    r = r_ref[...]
    xn = _rms(r, g_ref[...]).astype(BF16)
    gate = _sigmoid(_dot(xn, wg_ref[...]))
    pp = _dot(p_ref[...].astype(BF16), wp_ref[...])
    out = r + gate * pp
    if final:
        out = _rms(out, gf_ref[...])
    o_ref[...] = out


def _ple(r, g, w_gate, p, w_proj, g_final=None):
    m, d = r.shape
    pd = p.shape[1]
    tm = _pick_tile(m, (512, 256, 128))
    final = g_final is not None
    args = [r, g.reshape(1, d), w_gate, p, w_proj]
    specs = [
        pl.BlockSpec((tm, d), lambda i: (i, 0)),
        pl.BlockSpec((1, d), lambda i: (0, 0)),
        pl.BlockSpec((d, d), lambda i: (0, 0)),
        pl.BlockSpec((tm, pd), lambda i: (i, 0)),
        pl.BlockSpec((pd, d), lambda i: (0, 0)),
    ]
    if final:
        args.append(g_final.reshape(1, d))
        specs.append(pl.BlockSpec((1, d), lambda i: (0, 0)))
    return pl.pallas_call(
        functools.partial(_ple_kernel, final=final),
        grid=(m // tm,),
        in_specs=specs,
        out_specs=pl.BlockSpec((tm, d), lambda i: (i, 0)),
        out_shape=jax.ShapeDtypeStruct((m, d), F32),
        compiler_params=_cparams("parallel"),
        name="ple",
    )(*args)


def _sb_qkv_seq_kernel(*refs, heads, aliased):
    x_ref, g_ref, wqt_ref, wkt_ref, wvt_ref, wk_ref = refs[:6]
    kt_ref, vt_ref, qt_ref, vtb_ref, kh_ref = refs[6 + (2 if aliased else 0):]
    xn = _rms(x_ref[...], g_ref[...]).astype(BF16)
    qt_ref[...] = _dot_nt(wqt_ref[...], xn).astype(BF16)
    kt_ref[...] = _dot_nt(wkt_ref[...], xn)
    vt = _dot_nt(wvt_ref[...], xn)
    vt_ref[...] = vt
    vtb_ref[...] = vt.astype(BF16)
    k = _dot(xn, wk_ref[...])
    for h in range(heads):
        kh_ref[h] = k[:, h * SB_HEAD_DIM:(h + 1) * SB_HEAD_DIM].astype(BF16)


def _sb_qkv_seq(r, g, w_qkv, layer, n_layers, kt_all, vt_all, seq_len):
    m, d = r.shape
    heads = d // SB_HEAD_DIM
    tm = 256
    tps = seq_len // tm
    wq, wk, wv = w_qkv[:, :d], w_qkv[:, d:2 * d], w_qkv[:, 2 * d:]
    const = lambda i: (0, 0)
    wspec = pl.BlockSpec((d, d), const)
    aliased = kt_all is not None
    args = [r, g.reshape(1, d), wq.T, wk.T, wv.T, wk]
    specs = [pl.BlockSpec((tm, d), lambda i: (i, 0)), pl.BlockSpec((1, d), const), wspec, wspec, wspec, wspec]
    aliases = {}
    if aliased:
        args += [kt_all, vt_all]
        specs += [pl.BlockSpec(memory_space=pl.ANY), pl.BlockSpec(memory_space=pl.ANY)]
        aliases = {6: 0, 7: 1}
    layer_spec = pl.BlockSpec((None, None, d, tm), lambda i: (layer, i // tps, 0, i % tps))
    layer_shape = jax.ShapeDtypeStruct((n_layers, m // seq_len, d, seq_len), F32)
    t_spec = pl.BlockSpec((d, tm), lambda i: (0, i))
    return pl.pallas_call(
        functools.partial(_sb_qkv_seq_kernel, heads=heads, aliased=aliased),
        grid=(m // tm,),
        in_specs=specs,
        out_specs=[layer_spec, layer_spec, t_spec, t_spec,
                   pl.BlockSpec((heads, tm, SB_HEAD_DIM), lambda i: (0, i, 0))],
        out_shape=[layer_shape, layer_shape,
                   jax.ShapeDtypeStruct((d, m), BF16), jax.ShapeDtypeStruct((d, m), BF16),
                   jax.ShapeDtypeStruct((heads, m, SB_HEAD_DIM), BF16)],
        input_output_aliases=aliases,
        compiler_params=_cparams("parallel"),
        name="sb_qkv_seq",
    )(*args)


def _sublane_suffix_exclusive(x):
    idx = lax.broadcasted_iota(jnp.int32, x.shape, 0)
    y = jnp.where(idx + 1 < SUBLANES, pltpu.roll(x, SUBLANES - 1, 0), 0.0)
    y = y + jnp.where(idx + 1 < SUBLANES, pltpu.roll(y, SUBLANES - 1, 0), 0.0)
    y = y + jnp.where(idx + 2 < SUBLANES, pltpu.roll(y, SUBLANES - 2, 0), 0.0)
    y = y + jnp.where(idx + 4 < SUBLANES, pltpu.roll(y, SUBLANES - 4, 0), 0.0)
    return y


def _sb_seq_kernel(q_ref, k_ref, v_ref, o_ref, kp_ref, vp_ref, z_ref, e_ref, a_ref, f_ref, acc_ref, c_ref,
                   *, nq, hp):
    tb = SB_BLOCK
    dh = SB_HEAD_DIM
    seg_idx = lax.broadcasted_iota(jnp.int32, (SUBLANES, tb), 0) * SB_SEG
    lane_idx = lax.broadcasted_iota(jnp.int32, (SUBLANES, tb), 1)
    slot = lax.broadcasted_iota(jnp.int32, (tb, tb), 0)
    key = lax.broadcasted_iota(jnp.int32, (tb, tb), 1)
    perm01 = jnp.where(key == (slot & (SUBLANES - 1)) * SB_SEG + (slot >> 3), 1.0, 0.0).astype(BF16)

    for kt in range(nq):
        cols = slice(kt * tb, (kt + 1) * tb)
        for hs in range(hp):
            feat = slice(hs * dh, (hs + 1) * dh)
            kp_ref[hs, cols, :] = _dot(perm01, k_ref[hs, cols, :]).astype(BF16)
            vp_ref[feat, cols] = _dot_nt(v_ref[feat, cols], perm01).astype(BF16)


    def scores(qts, kj):
        k0 = pl.multiple_of(kj * tb, tb)
        return [_dot(kp_ref[hs, pl.ds(k0, tb), :], qts[hs]) for hs in range(hp)]

    def weighted_values(slot, kj):
        k0 = pl.multiple_of(kj * tb, tb)
        return [_dot(vp_ref[hs * dh:(hs + 1) * dh, pl.ds(k0, tb)], a_ref[slot, hs]) for hs in range(hp)]

    def accumulate(slot, parts):
        for hs in range(hp):
            acc_ref[hs * dh:(hs + 1) * dh, :] += parts[hs] * f_ref[slot, hs:hs + 1, :]

    def weights(slot, hs, diag):
        run = jnp.zeros((SUBLANES, tb), F32)
        for r in range(SB_SEG - 1, -1, -1):
            rows = slice(r * SUBLANES, (r + 1) * SUBLANES)
            zr = z_ref[slot, hs, rows, :]
            sp = _softplus(zr)
            if diag:
                sp = jnp.where(seg_idx + r < lane_idx, sp, 0.0)
            run = run + sp
            e_ref[hs, rows, :] = jnp.exp(zr - run)
        later = _sublane_suffix_exclusive(run)
        ecv = jnp.exp(-later)
        for r in range(0, SB_SEG, 2):
            parts = []
            for rr in (r, r + 1):
                a = e_ref[hs, rr * SUBLANES:(rr + 1) * SUBLANES, :] * ecv
                if diag:
                    a = jnp.where(seg_idx + rr < lane_idx, a, 0.0)
                parts.append(a)
            a_ref[slot, hs, r * SUBLANES:(r + 2) * SUBLANES, :] = jnp.concatenate(parts, axis=0).astype(BF16)
        return (later + run)[0:1, :]

    def q_body(qi, _):
        q0 = pl.multiple_of(qi * tb, tb)
        qts = [(q_ref[hs * dh:(hs + 1) * dh, pl.ds(q0, tb)].astype(F32) * (dh ** -0.5)).astype(BF16)
               for hs in range(hp)]
        z_first = scores(qts, qi)
        z_next = scores(qts, jnp.maximum(qi - 1, 0))
        acc_ref[...] = jnp.zeros_like(acc_ref)
        for hs in range(hp):
            z_ref[0, hs] = z_first[hs]
            z_ref[1, hs] = z_next[hs]
        for hs in range(hp):
            c_ref[hs:hs + 1, :] = weights(0, hs, True)
            f_ref[0, hs:hs + 1, :] = jnp.ones((1, tb), F32)

        def step(s, slot, prefetch=True):
            prev = 1 - slot
            parts = weighted_values(prev, qi - s + 1)
            if prefetch:
                z_next = scores(qts, jnp.maximum(qi - s - 1, 0))
                for hs in range(hp):
                    z_ref[prev, hs] = z_next[hs]
            for hs in range(hp):
                c_old = c_ref[hs:hs + 1, :]
                total = weights(slot, hs, False)
                f_ref[slot, hs:hs + 1, :] = jnp.exp(-c_old)
                c_ref[hs:hs + 1, :] = c_old + total
            accumulate(prev, parts)

        def pair_body(p, _):
            step(2 * p + 1, 1)
            step(2 * p + 2, 0)
            return 0

        lax.fori_loop(0, qi // 2, pair_body, 0)

        @pl.when(qi % 2 == 1)
        def _():
            step(qi, 1, prefetch=False)
            accumulate(1, weighted_values(1, 0))

        @pl.when(qi % 2 == 0)
        def _():
            accumulate(0, weighted_values(0, 0))

        o_ref[pl.ds(q0, tb), :] = jnp.transpose(acc_ref[...]).astype(o_ref.dtype)
        return 0

    lax.fori_loop(0, nq, q_body, 0)


def _sb_seq(q_t, k_h, v_t, batch, seq_len):
    d, m = q_t.shape
    heads = d // SB_HEAD_DIM
    hp = SB_HEADS_PER_STEP
    nq = seq_len // SB_BLOCK
    t_spec = pl.BlockSpec((hp * SB_HEAD_DIM, seq_len), lambda b, j: (j, b))
    return pl.pallas_call(
        functools.partial(_sb_seq_kernel, nq=nq, hp=hp),
        grid=(batch, heads // hp),
        in_specs=[t_spec, pl.BlockSpec((hp, seq_len, SB_HEAD_DIM), lambda b, j: (j, b, 0)), t_spec],
        out_specs=pl.BlockSpec((seq_len, hp * SB_HEAD_DIM), lambda b, j: (b, j)),
        out_shape=jax.ShapeDtypeStruct((m, d), BF16),
        scratch_shapes=[pltpu.VMEM((hp, seq_len, SB_HEAD_DIM), BF16),
                        pltpu.VMEM((hp * SB_HEAD_DIM, seq_len), BF16),
                        pltpu.VMEM((2, hp, SB_BLOCK, SB_BLOCK), F32),
                        pltpu.VMEM((hp, SB_BLOCK, SB_BLOCK), F32),
                        pltpu.VMEM((2, hp, SB_BLOCK, SB_BLOCK), BF16),
                        pltpu.VMEM((2, SUBLANES, SB_BLOCK), F32),
                        pltpu.VMEM((hp * SB_HEAD_DIM, SB_BLOCK), F32),
                        pltpu.VMEM((SUBLANES, SB_BLOCK), F32)],
        compiler_params=_cparams("parallel", "parallel"),
        name="sb_seq",
    )(q_t, k_h, v_t)


def _sb_step_kernel(pt_ref, q_ref, *refs, heads, pps):
    k_refs = refs[:pps]
    v_refs = refs[pps:2 * pps]
    o_ref, qb_ref, acc_ref, carry_ref = refs[2 * pps:]
    p = pl.program_id(1)
    d = heads * SB_HEAD_DIM
    own = (lax.broadcasted_iota(jnp.int32, (heads, d), 1) // SB_HEAD_DIM
           == lax.broadcasted_iota(jnp.int32, (heads, d), 0))

    @pl.when(p == 0)
    def _():
        q = jnp.broadcast_to(q_ref[...], (heads, d)) * (SB_HEAD_DIM ** -0.5)
        qb_ref[...] = jnp.where(own, q, 0.0).astype(BF16)
        acc_ref[...] = jnp.zeros_like(acc_ref)
        carry_ref[...] = jnp.zeros_like(carry_ref)

    row = lax.broadcasted_iota(jnp.int32, (PAGE, PAGE), 0)
    col = lax.broadcasted_iota(jnp.int32, (PAGE, PAGE), 1)
    suffix01 = jnp.where(row >= col, 1.0, 0.0).astype(BF16)
    qb = qb_ref[...]
    zs = [_dot(qb, k_refs[u][...].astype(BF16)) for u in range(pps)]
    splits = [_split3(_softplus(z)) for z in zs]
    cums = [_dot(hi, suffix01) + _dot(mid, suffix01) + _dot(lo, suffix01)
            for hi, mid, lo in splits]
    carries = [carry_ref[...]]
    for u in range(pps):
        carries.append(carries[u] + cums[u][:, 0:1])
    ws = [jnp.exp(zs[u] - cums[u] - carries[u]).astype(BF16) for u in range(pps)]
    parts = [_dot_nt(ws[u], v_refs[u][...].astype(BF16)) for u in range(pps)]
    acc = acc_ref[...]
    for part in parts:
        acc = acc + part
    acc_ref[...] = acc
    carry_ref[...] = carries[pps]

    @pl.when(p == pl.num_programs(1) - 1)
    def _():
        o_ref[...] = jnp.sum(jnp.where(own, acc, 0.0), axis=0, keepdims=True)


def _sb_step(q, page_table, cache_k, cache_v, layer):
    b, d = q.shape
    n_pages = page_table.shape[1]
    heads = d // SB_HEAD_DIM
    pps = _pick_tile(n_pages, (SB_PAGES_PER_STEP, 4, 2, 1))

    def page_spec(u):
        return pl.BlockSpec((None, None, d, PAGE),
                            lambda i, p, pt: (layer, pt[i, n_pages - 1 - (p * pps + u)], 0, 0))

    row_spec = pl.BlockSpec((None, 1, d), lambda i, p, pt: (i, 0, 0))
    page_specs = [page_spec(u) for u in range(pps)]
    grid_spec = pltpu.PrefetchScalarGridSpec(
        num_scalar_prefetch=1,
        grid=(b, n_pages // pps),
        in_specs=[row_spec] + page_specs + page_specs,
        out_specs=row_spec,
        scratch_shapes=[pltpu.VMEM((heads, d), BF16), pltpu.VMEM((heads, d), F32),
                        pltpu.VMEM((heads, 1), F32)],
    )
    out = pl.pallas_call(
        functools.partial(_sb_step_kernel, heads=heads, pps=pps),
        grid_spec=grid_spec,
        out_shape=jax.ShapeDtypeStruct((b, 1, d), F32),
        compiler_params=_cparams("parallel", "arbitrary"),
        name="sb_step",
    )(page_table, q.reshape(b, 1, d), *([cache_k] * pps), *([cache_v] * pps))
    return out.reshape(b, d)


def _gmlp_kernel(*refs, single, half, tm):
    if single:
        (x_ref, g_ref, win_ref, bin_ref, lng_ref, lnb_ref, wsv_ref, bsv_ref, wo_ref,
         o_ref, vout_ref, vn_ref, acc_ref) = refs
    else:
        (x_ref, g_ref, win_ref, bin_ref, lng_ref, lnb_ref, ws_ref, bst_ref, wo_ref,
         o_ref, vn_ref, acc_ref) = refs
    gd = half // CM_GROUPS
    pair = 2 * gd
    x = x_ref[...]
    xn = _rms(x, g_ref[...]).astype(BF16)

    v = _gelu(_dot(xn, win_ref[:, half:]) + bin_ref[:, half:])
    mu = jnp.mean(v, axis=-1, keepdims=True)
    var = jnp.mean(jnp.square(v - mu), axis=-1, keepdims=True)
    vn = (v - mu) * lax.rsqrt(var + EPS) * lng_ref[...] + lnb_ref[...]
    if single:
        vout_ref[...] = vn
        vn_ref[...] = vn
    else:
        vn_ref[...] = vn.astype(BF16)
        row = lax.broadcasted_iota(jnp.int32, (CM_CHUNK, CM_CHUNK), 0)
        col = lax.broadcasted_iota(jnp.int32, (CM_CHUNK, CM_CHUNK), 1)
        causal = col <= row

    for gp in range(CM_GROUPS // 2):
        lo = gp * pair
        u = _gelu(_dot(xn, win_ref[:, lo:lo + pair]) + bin_ref[:, lo:lo + pair])
        if single:
            s = vn_ref[:, lo:lo + pair] * wsv_ref[:, lo:lo + pair] + bsv_ref[:, lo:lo + pair]
        else:
            cols = []
            for gi in range(2):
                g = 2 * gp + gi
                ws = jnp.where(causal, ws_ref[g], 0.0).astype(BF16)
                bs = bst_ref[:, g:g + 1]
                rows = []
                for c in range(tm // CM_CHUNK):
                    vc = vn_ref[c * CM_CHUNK:(c + 1) * CM_CHUNK, g * gd:(g + 1) * gd]
                    rows.append(_dot(ws, vc) + bs)
                cols.append(jnp.concatenate(rows, axis=0) if len(rows) > 1 else rows[0])
            s = jnp.concatenate(cols, axis=1)
        y = (u * s).astype(BF16)
        contrib = _dot(y, wo_ref[lo:lo + pair, :])
        if gp == 0:
            acc_ref[...] = contrib
        else:
            acc_ref[...] += contrib
    o_ref[...] = x + acc_ref[...]


def _gmlp(r, g, w_in, b_in, ln_g, ln_b, w_s, b_s, w_o, single):
    m, d = r.shape
    half = w_o.shape[0]
    gd = half // CM_GROUPS
    tm = min(m, 128) if single else 256
    const = lambda i: (0, 0)
    specs = [
        pl.BlockSpec((tm, d), lambda i: (i, 0)),
        pl.BlockSpec((1, d), const),
        pl.BlockSpec((d, 2 * half), const),
        pl.BlockSpec((1, 2 * half), const),
        pl.BlockSpec((1, half), const),
        pl.BlockSpec((1, half), const),
    ]
    args = [r, g.reshape(1, d), w_in, b_in.reshape(1, -1), ln_g.reshape(1, -1), ln_b.reshape(1, -1)]
    if single:
        args += [jnp.repeat(w_s[:, 0, 0], gd).reshape(1, half), jnp.repeat(b_s[:, 0], gd).reshape(1, half)]
        specs += [pl.BlockSpec((1, half), const), pl.BlockSpec((1, half), const)]
    else:
        args += [w_s, jnp.transpose(b_s)]
        specs += [pl.BlockSpec((CM_GROUPS, CM_CHUNK, CM_CHUNK), lambda i: (0, 0, 0)),
                  pl.BlockSpec((CM_CHUNK, CM_GROUPS), const)]
    args.append(w_o)
    specs.append(pl.BlockSpec((half, d), const))
    row_out = pl.BlockSpec((tm, d), lambda i: (i, 0))
    if single:
        out_specs = [row_out, pl.BlockSpec((tm, half), lambda i: (i, 0))]
        out_shape = [jax.ShapeDtypeStruct((m, d), F32), jax.ShapeDtypeStruct((m, half), F32)]
        scratch = [pltpu.VMEM((tm, half), F32), pltpu.VMEM((tm, d), F32)]
    else:
        out_specs = row_out
        out_shape = jax.ShapeDtypeStruct((m, d), F32)
        scratch = [pltpu.VMEM((tm, half), BF16), pltpu.VMEM((tm, d), F32)]
    return pl.pallas_call(
        functools.partial(_gmlp_kernel, single=single, half=half, tm=tm),
        grid=(m // tm,),
        in_specs=specs,
        out_specs=out_specs,
        out_shape=out_shape,
        scratch_shapes=scratch,
        compiler_params=_cparams("parallel"),
        name="gmlp_step" if single else "gmlp_seq",
    )(*args)


def _ml_qkv_tail(xc, x, wq_ref, wk_ref, wv_ref, q_ref, k_ref, v_ref, scale):
    xc = xc.astype(BF16)
    q_ref[...] = _dot(xc, wq_ref[...]).astype(q_ref.dtype)
    k_ref[...] = (_dot(xc, wk_ref[...]) * scale).astype(k_ref.dtype)
    v_ref[...] = _dot(x.astype(BF16), wv_ref[...]).astype(v_ref.dtype)


def _ml_qkv_seq_kernel(x_ref, halo_ref, cw_ref, cb_ref, wq_ref, wk_ref, wv_ref, q_ref, k_ref, v_ref, xs_ref,
                       *, tiles_per_seq, tm, scale):
    i = pl.program_id(1)
    x = x_ref[...]
    xs_ref[:ML_HALO, :] = jnp.where(i % tiles_per_seq == 0, 0.0, halo_ref[...])
    xs_ref[ML_HALO:, :] = x
    y = cb_ref[...]
    for t in range(ML_CONV - 1):
        off = ML_HALO - (ML_CONV - 1) + t
        y = y + cw_ref[t:t + 1, :] * xs_ref[off:off + tm, :]
    y = y + cw_ref[ML_CONV - 1:ML_CONV, :] * x
    _ml_qkv_tail(_silu(y), x, wq_ref, wk_ref, wv_ref, q_ref, k_ref, v_ref, scale)


def _ml_qkv_step_kernel(x_ref, b0_ref, b1_ref, b2_ref, cw_ref, cb_ref, wq_ref, wk_ref, wv_ref,
                        q_ref, k_ref, v_ref, *, scale):
    x = x_ref[...]
    y = (cb_ref[...] + cw_ref[0:1, :] * b0_ref[...] + cw_ref[1:2, :] * b1_ref[...]
         + cw_ref[2:3, :] * b2_ref[...] + cw_ref[3:4, :] * x)
    _ml_qkv_tail(_silu(y), x, wq_ref, wk_ref, wv_ref, q_ref, k_ref, v_ref, scale)


def _ml_qkv(proj, conv_w, conv_b, w_q, w_k, w_v, *, seq_len=None, bufs=None, qv_dtype=BF16):
    m = proj.shape[0]
    nh, hd, _ = w_q.shape
    inner = nh * hd
    scale = hd ** -0.5
    cb = conv_b.reshape(1, inner)
    wspec = pl.BlockSpec((None, hd, hd), lambda h, i: (h, 0, 0))
    cwspec = pl.BlockSpec((ML_CONV, hd), lambda h, i: (0, h))
    cbspec = pl.BlockSpec((1, hd), lambda h, i: (0, h))
    if bufs is None:
        tm = _pick_tile(seq_len, (512, 256, 128))
        hb = tm // ML_HALO
        xspec = pl.BlockSpec((tm, hd), lambda h, i: (i, h))
        kern = functools.partial(_ml_qkv_seq_kernel, tiles_per_seq=seq_len // tm, tm=tm, scale=scale)
        args = [proj, proj, conv_w, cb, w_q, w_k, w_v]
        specs = [xspec, pl.BlockSpec((ML_HALO, hd), lambda h, i: (jnp.maximum(i * hb - 1, 0), h)),
                 cwspec, cbspec, wspec, wspec, wspec]
        scratch = [pltpu.VMEM((tm + ML_HALO, hd), F32)]
    else:
        tm = m
        xspec = pl.BlockSpec((tm, hd), lambda h, i: (i, h))
        kern = functools.partial(_ml_qkv_step_kernel, scale=scale)
        args = [proj, *bufs, conv_w, cb, w_q, w_k, w_v]
        specs = [xspec, xspec, xspec, xspec, cwspec, cbspec, wspec, wspec, wspec]
        scratch = []
    return pl.pallas_call(
        kern,
        grid=(nh, m // tm),
        in_specs=specs,
        out_specs=[xspec, xspec, xspec],
        out_shape=[jax.ShapeDtypeStruct((m, inner), qv_dtype), jax.ShapeDtypeStruct((m, inner), F32),
                   jax.ShapeDtypeStruct((m, inner), qv_dtype)],
        scratch_shapes=scratch,
        compiler_params=_cparams("parallel", "parallel"),
        name="ml_qkv",
    )(*args)


def _lane_select(x, idx):
    lane = lax.broadcasted_iota(jnp.int32, x.shape, 1)
    return jnp.sum(jnp.where(lane == idx, x, 0.0), axis=1, keepdims=True)


def _ml_chunk_kernel(q_ref, k_ref, v_ref, gt_ref, h_ref, c_out, n_out, m_out, c_ref, n_ref, m_ref, *, heads):
    c = pl.program_id(1)
    L = ML_CHUNK
    hd = q_ref.shape[1] // heads
    hs = range(heads)
    cols = [slice(j * hd, (j + 1) * hd) for j in hs]

    @pl.when(c == 0)
    def _():
        c_ref[...] = jnp.zeros_like(c_ref)
        n_ref[...] = jnp.zeros_like(n_ref)
        m_ref[...] = jnp.zeros_like(m_ref)

    gt = gt_ref[...]
    row = lax.broadcasted_iota(jnp.int32, (L, L), 0)
    col = lax.broadcasted_iota(jnp.int32, (L, L), 1)
    causal = col <= row
    lower01 = jnp.where(causal, 1.0, 0.0).astype(BF16)
    ones01 = jnp.ones((L, L), BF16)
    i_col = [_lane_select(gt, j) for j in hs]
    f_col = [_log_sigmoid(_lane_select(gt, j + heads)) for j in hs]
    f_b = [jnp.broadcast_to(f_col[j], (L, L)) for j in hs]
    i_b = [jnp.broadcast_to(i_col[j], (L, L)) for j in hs]
    parts_c = [_split3(f_b[j]) for j in hs]
    parts_r = [_split3(jnp.where(row <= col, f_b[j], 0.0)) for j in hs]
    parts_i = [_split3(jnp.where(row == col, i_b[j], 0.0)) for j in hs]
    dot3 = lambda lhs, p: _dot(lhs, p[0]) + _dot(lhs, p[1]) + _dot(lhs, p[2])
    b_c = [dot3(lower01, parts_c[j]) for j in hs]
    b_r = [dot3(ones01, parts_r[j]) for j in hs]
    i_r = [dot3(ones01, parts_i[j]) for j in hs]

    b_col = [b_c[j][:, 0:1] for j in hs]
    m_prev = [m_ref[j, 0:1, 0:1] for j in hs]
    d_log = [jnp.where(causal, b_c[j] - b_r[j] + i_r[j], -jnp.inf) for j in hs]
    inter = [b_col[j] + m_prev[j] for j in hs]
    m_t = [jnp.maximum(inter[j], jnp.max(d_log[j], axis=1, keepdims=True)) for j in hs]
    w_intra = [jnp.exp(d_log[j] - m_t[j]) for j in hs]
    w_inter = [jnp.exp(inter[j] - m_t[j]) for j in hs]

    q = [q_ref[:, cols[j]].astype(BF16) for j in hs]
    k = [k_ref[:, cols[j]] for j in hs]
    kb = [k[j].astype(BF16) for j in hs]
    v = [v_ref[:, cols[j]].astype(BF16) for j in hs]
    c_old = [c_ref[j] for j in hs]
    n_old = [n_ref[j] for j in hs]
    s = [_dot_nt(q[j], kb[j]) for j in hs]
    q_c = [_dot(q[j], c_old[j].astype(BF16)) for j in hs]
    q_n = [_dot_nt(q[j], n_old[j].astype(BF16))[:, 0:1] for j in hs]
    qk = [s[j] * w_intra[j] for j in hs]
    qk_v = [_dot(qk[j].astype(BF16), v[j]) for j in hs]
    for j in hs:
        num = w_inter[j] * q_c[j] + qk_v[j]
        den = w_inter[j] * q_n[j] + jnp.sum(qk[j], axis=1, keepdims=True)
        h_ref[:, cols[j]] = num / jnp.maximum(jnp.abs(den), jnp.exp(-m_t[j]))

    m_new = [m_t[j][L - 1:L, :] for j in hs]
    b_last = [b_col[j][L - 1:L, :] for j in hs]
    w_state = [jnp.exp(b_last[j] + m_prev[j] - m_new[j]) for j in hs]
    w_rows_c = [jnp.exp(b_last[j] - b_col[j] + i_col[j] - m_new[j]) for j in hs]
    w_rows_r = [jnp.exp(b_last[j] - b_r[j][0:SUBLANES, :] + i_r[j][0:SUBLANES, :] - m_new[j])
                for j in hs]
    kw = [(k[j] * w_rows_c[j]).astype(BF16) for j in hs]
    c_upd = [_dot_tn(kw[j], v[j]) for j in hs]
    n_upd = [_dot(w_rows_r[j].astype(BF16), kb[j]) for j in hs]
    for j in hs:
        c_ref[j] = w_state[j] * c_old[j] + c_upd[j]
        n_ref[j] = w_state[j] * n_old[j] + n_upd[j]
        m_ref[j] = jnp.broadcast_to(m_new[j], m_ref.shape[1:])

    @pl.when(c == pl.num_programs(1) - 1)
    def _():
        c_out[...] = c_ref[...]
        for j in hs:
            n_out[j] = n_ref[j, 0:1, :]
            m_out[j] = m_ref[j, 0:1, :]


def _ml_chunks(q, k, v, gates, batch, seq_len, heads):
    m, inner = q.shape
    hd = inner // heads
    nc = seq_len // ML_CHUNK
    xspec = pl.BlockSpec((ML_CHUNK, inner), lambda b, c: (b * nc + c, 0))
    return pl.pallas_call(
        functools.partial(_ml_chunk_kernel, heads=heads),
        grid=(batch, nc),
        in_specs=[xspec, xspec, xspec, pl.BlockSpec((ML_CHUNK, LANES), lambda b, c: (b * nc + c, 0))],
        out_specs=[xspec,
                   pl.BlockSpec((None, heads, hd, hd), lambda b, c: (b, 0, 0, 0)),
                   pl.BlockSpec((None, heads, 1, hd), lambda b, c: (b, 0, 0, 0)),
                   pl.BlockSpec((None, heads, 1, LANES), lambda b, c: (b, 0, 0, 0))],
        out_shape=[jax.ShapeDtypeStruct((m, inner), F32),
                   jax.ShapeDtypeStruct((batch, heads, hd, hd), F32),
                   jax.ShapeDtypeStruct((batch, heads, 1, hd), F32),
                   jax.ShapeDtypeStruct((batch, heads, 1, LANES), F32)],
        scratch_shapes=[pltpu.VMEM((heads, hd, hd), F32), pltpu.VMEM((heads, SUBLANES, hd), F32),
                        pltpu.VMEM((heads, SUBLANES, LANES), F32)],
        compiler_params=_cparams("parallel", "arbitrary"),
        name="ml_chunks",
    )(q, k, v, gates)


def _ml_step_kernel(q_ref, k_ref, v_ref, gt_ref, mp_ref, c_ref, n_ref, h_ref, c_out, n_out, m_out, *, heads):
    gt = gt_ref[...]
    mp = mp_ref[...]
    hd = q_ref.shape[1] // heads
    lane = lax.broadcasted_iota(jnp.int32, (1, LANES), 1)
    first_row = lax.broadcasted_iota(jnp.int32, (SUBLANES, hd), 0) == 0
    m_all = jnp.zeros((1, LANES), F32)
    for j in range(heads):
        cols = slice(j * hd, (j + 1) * hd)
        i_g = _lane_select(gt, j)
        f_g = _log_sigmoid(_lane_select(gt, j + heads))
        inter = f_g + _lane_select(mp, j)
        m_t = jnp.maximum(inter, i_g)
        w_in = jnp.exp(i_g - m_t)
        w_st = jnp.exp(inter - m_t)

        q = q_ref[:, cols]
        k = k_ref[:, cols]
        v = v_ref[:, cols]
        n = n_ref[:, cols]
        c_old = c_ref[j]
        q8 = jnp.broadcast_to(q, (SUBLANES, hd)).astype(BF16)
        q_c = _dot(q8, c_old.astype(BF16))[0:1, :]
        qk = jnp.sum(q * k, axis=1, keepdims=True) * w_in
        num = w_st * q_c + qk * v
        den = w_st * jnp.sum(q * n, axis=1, keepdims=True) + qk
        h_ref[:, cols] = num / jnp.maximum(jnp.abs(den), jnp.exp(-m_t))

        kw = k * w_in
        kw8 = jnp.where(first_row, jnp.broadcast_to(kw, (SUBLANES, hd)), 0.0).astype(BF16)
        v8 = jnp.broadcast_to(v, (SUBLANES, hd)).astype(BF16)
        c_out[j] = w_st * c_old + _dot_tn(kw8, v8)
        n_out[:, cols] = w_st * n + kw
        m_all = jnp.where(lane == j, m_t, m_all)
    m_out[...] = m_all


def _ml_step(q, k, v, gates, m_prev, c_state, n_state, heads):
    b, inner = q.shape
    hd = inner // heads
    r3 = lambda a: a.reshape(b, 1, -1)
    vspec = pl.BlockSpec((None, 1, inner), lambda i: (i, 0, 0))
    gspec = pl.BlockSpec((None, 1, LANES), lambda i: (i, 0, 0))
    cspec = pl.BlockSpec((None, heads, hd, hd), lambda i: (i, 0, 0, 0))
    return pl.pallas_call(
        functools.partial(_ml_step_kernel, heads=heads),
        grid=(b,),
        in_specs=[vspec, vspec, vspec, gspec, gspec, cspec, vspec],
        out_specs=[vspec, cspec, vspec, gspec],
        out_shape=[jax.ShapeDtypeStruct((b, 1, inner), F32),
                   jax.ShapeDtypeStruct((b, heads, hd, hd), F32),
                   jax.ShapeDtypeStruct((b, 1, inner), F32),
                   jax.ShapeDtypeStruct((b, 1, LANES), F32)],
        compiler_params=_cparams("parallel"),
        name="ml_step",
    )(r3(q), r3(k), r3(v), r3(gates), r3(m_prev), c_state, r3(n_state))


def _ml_out_kernel(h_ref, o_ref, ng_ref, wo_ref, r_ref, out_ref, *, heads):
    h = h_ref[...]
    hd = h.shape[1] // heads
    parts = []
    for j in range(heads):
        hh = h[:, j * hd:(j + 1) * hd]
        mu = jnp.mean(hh, axis=-1, keepdims=True)
        var = jnp.mean(jnp.square(hh - mu), axis=-1, keepdims=True)
        parts.append((hh - mu) * lax.rsqrt(var + EPS))
    hn = jnp.concatenate(parts, axis=1) * ng_ref[...]
    out = (_sigmoid(o_ref[...]) * hn).astype(BF16)
    out_ref[...] = r_ref[...] + _dot(out, wo_ref[...])


def _ml_out(h, proj, norm_g, w_o, r, heads):
    m, inner = h.shape
    d = r.shape[1]
    tm = _pick_tile(m, (256, 128))
    return pl.pallas_call(
        functools.partial(_ml_out_kernel, heads=heads),
        grid=(m // tm,),
        in_specs=[pl.BlockSpec((tm, inner), lambda i: (i, 0)),
                  pl.BlockSpec((tm, inner), lambda i: (i, 1)),
                  pl.BlockSpec((1, inner), lambda i: (0, 0)),
                  pl.BlockSpec((inner, d), lambda i: (0, 0)),
                  pl.BlockSpec((tm, d), lambda i: (i, 0))],
        out_specs=pl.BlockSpec((tm, d), lambda i: (i, 0)),
        out_shape=jax.ShapeDtypeStruct((m, d), F32),
        compiler_params=_cparams("parallel"),
        name="ml_out",
    )(h, proj, norm_g.reshape(1, inner), w_o, r)


def _trunk(x, p, seq_len, weights, *, past=None, ml_state=None, ffn_state=None):
    (norm_mix, norm_ffn, norm_ple, norm_final, sb_w_qkv, sb_w_o, cm_w_in, cm_b_in, cm_ln_g, cm_ln_b,
     cm_w_s, cm_b_s, cm_w_o, ml_w_in, ml_b_gates, ml_conv_w, ml_conv_b, ml_w_q, ml_w_k, ml_w_v,
     ml_norm_g, ml_w_o, ffn_w_up, ffn_conv_w, ffn_conv_b, ffn_w_down, ple_w_proj, ple_w_gate) = weights
    m, d = x.shape
    batch = m // seq_len
    depth = norm_mix.shape[0]
    n_sb = sb_w_qkv.shape[0]
    step = seq_len == 1
    heads = d // SB_HEAD_DIM
    outs = dict(k=[], v=[], cm=[], c=[], n=[], m=[], conv=[], ffn=[])
    kt_all = vt_all = None
    r = x
    for i in range(depth):
        kind, j = i % 3, i // 3
        if kind == 0:
            if step:
                qkv = _matmul(r, sb_w_qkv[j], g=norm_mix[i], name="sb_qkv")
                q, k, v = qkv[:, :d], qkv[:, d:2 * d], qkv[:, 2 * d:]
                outs["k"].append(k.reshape(batch, seq_len, heads, SB_HEAD_DIM))
                outs["v"].append(v.reshape(batch, seq_len, heads, SB_HEAD_DIM))
                page_table, cache_k, cache_v = past
                o = _sb_step(q, page_table, cache_k, cache_v, j)
            else:
                kt_all, vt_all, q_t, v_t, k_h = _sb_qkv_seq(r, norm_mix[i], sb_w_qkv[j], j, n_sb, kt_all, vt_all,
                                                           seq_len)
                o = _sb_seq(q_t, k_h, v_t, batch, seq_len)
            if step:
                r = _matmul(o, sb_w_o[j], res=r, name="sb_out")
            else:
                r = _matmul_rows(o, sb_w_o[j], tm=512, res=r, name="sb_out")
        elif kind == 1:
            res = _gmlp(r, norm_mix[i], cm_w_in[j], cm_b_in[j], cm_ln_g[j], cm_ln_b[j], cm_w_s[j], cm_b_s[j],
                        cm_w_o[j], step)
            if step:
                r, vn = res
                outs["cm"].append(vn.reshape(batch, 1, -1))
            else:
                r = res
        else:
            nh = ML_HEADS
            inner = ml_w_q.shape[2] * nh
            w_main = ml_w_in[j][:, :2 * inner]
            w_gate = jnp.pad(ml_w_in[j][:, 2 * inner:], ((0, 0), (0, LANES - 2 * nh)))
            b_gate = jnp.pad(ml_b_gates[j], (0, LANES - 2 * nh))
            if step:
                proj = _matmul(r, w_main, g=norm_mix[i], name="ml_in")
            else:
                proj = _matmul_rows(r, w_main, tm=256, g=norm_mix[i], name="ml_in")
            gates = _matmul(r, w_gate, g=norm_mix[i], bias=b_gate, name="ml_gates")
            if step:
                c0, n0, m0, conv0 = ml_state
                bufs = [conv0[j][:, t] for t in range(ML_CONV - 1)]
                q, k, v = _ml_qkv(proj, ml_conv_w[j], ml_conv_b[j], ml_w_q[j], ml_w_k[j], ml_w_v[j],
                                  bufs=bufs, qv_dtype=F32)
                m_prev = jnp.pad(m0[j], ((0, 0), (0, LANES - nh)))
                hh, c_new, n_new, m_new = _ml_step(q, k, v, gates, m_prev, c0[j], n0[j], nh)
                hh = hh.reshape(m, inner)
                m_new = m_new[:, 0, :nh]
                outs["conv"].append(jnp.concatenate([conv0[j][:, 1:], proj[:, None, :inner]], axis=1))
            else:
                q, k, v = _ml_qkv(proj, ml_conv_w[j], ml_conv_b[j], ml_w_q[j], ml_w_k[j], ml_w_v[j],
                                  seq_len=seq_len)
                hh, c_new, n_new, m_new = _ml_chunks(q, k, v, gates, batch, seq_len, nh)
                m_new = m_new[:, :, 0, 0]
                tail = proj.reshape(batch, seq_len, -1)[:, seq_len - (ML_CONV - 1):, :inner]
                outs["conv"].append(tail)
            outs["c"].append(c_new)
            outs["n"].append(n_new.reshape(batch, nh, -1))
            outs["m"].append(m_new)
            r = _ml_out(hh, proj, ml_norm_g[j], ml_w_o[j], r, nh)

        g_final = norm_final if i == depth - 1 else None
        if step:
            a = _matmul(r, ffn_w_up[i], g=norm_ffn[i], name="ffn_up")
            buf = ffn_state[i]
            r = _ffn_step(r, a, buf[:, 0], buf[:, 1], ffn_conv_w[i], ffn_conv_b[i], ffn_w_down[i])
            outs["ffn"].append(jnp.concatenate([buf[:, 1:], a[:, None, :]], axis=1))
            r = _ple(r, norm_ple[i], ple_w_gate[i], p[i], ple_w_proj[i], g_final=g_final)
        else:
            tail = r.reshape(batch, seq_len, d)[:, seq_len - (FFN_CONV - 1):].reshape(-1, d)
            a_tail = _matmul(tail, ffn_w_up[i], g=norm_ffn[i], name="ffn_tail")
            outs["ffn"].append(a_tail.reshape(batch, FFN_CONV - 1, -1))
            r = _ffn_ple_seq(r, norm_ffn[i], ffn_w_up[i], ffn_conv_w[i], ffn_conv_b[i], ffn_w_down[i], seq_len,
                             norm_ple[i], ple_w_gate[i], p[i], ple_w_proj[i], g_final=g_final)
    if not step:
        rows = lambda a: a.reshape(n_sb, batch, heads, SB_HEAD_DIM, seq_len).transpose(0, 1, 4, 2, 3)
        outs["k"], outs["v"] = rows(kt_all), rows(vt_all)
    else:
        outs["k"], outs["v"] = jnp.stack(outs["k"]), jnp.stack(outs["v"])
    return r, outs


def kernel(x_prompt, x_sample, p_prompt, p_sample, page_table, cache_k, cache_v, state_mlstm_c, state_mlstm_n, state_mlstm_m, state_mlstm_conv, state_ffn_conv, norm_mix, norm_ffn, norm_ple, norm_final, sb_w_qkv, sb_w_o, cm_w_in, cm_b_in, cm_ln_g, cm_ln_b, cm_w_s, cm_b_s, cm_w_o, ml_w_in, ml_b_gates, ml_conv_w, ml_conv_b, ml_w_q, ml_w_k, ml_w_v, ml_norm_g, ml_w_o, ffn_w_up, ffn_conv_w, ffn_conv_b, ffn_w_down, ple_w_proj, ple_w_gate):
    bf = lambda w: w.astype(BF16)
    weights = (norm_mix, norm_ffn, norm_ple, norm_final, bf(sb_w_qkv), bf(sb_w_o), bf(cm_w_in), cm_b_in,
               cm_ln_g, cm_ln_b, cm_w_s, cm_b_s, bf(cm_w_o), bf(ml_w_in), ml_b_gates, ml_conv_w, ml_conv_b,
               bf(ml_w_q), bf(ml_w_k), bf(ml_w_v), ml_norm_g, bf(ml_w_o), bf(ffn_w_up), ffn_conv_w,
               ffn_conv_b, bf(ffn_w_down), bf(ple_w_proj), bf(ple_w_gate))
    b, t, d = x_prompt.shape
    bs, ts, _ = x_sample.shape
    depth = p_prompt.shape[0]
    n_sb, n_phys, page, sbh, sbd = cache_k.shape
    pages = lambda c: c.transpose(0, 1, 3, 4, 2).reshape(n_sb, n_phys, sbh * sbd, page)

    y_p, o_p = _trunk(x_prompt.reshape(b * t, d), p_prompt.reshape(depth, b * t, -1), t, weights)
    y_s, o_s = _trunk(
        x_sample.reshape(bs * ts, d), p_sample.reshape(depth, bs * ts, -1), ts, weights,
        past=(page_table, pages(cache_k), pages(cache_v)),
        ml_state=(state_mlstm_c, state_mlstm_n, state_mlstm_m, state_mlstm_conv),
        ffn_state=state_ffn_conv)

    st = jnp.stack
    return (y_p.reshape(b, t, d), y_s.reshape(bs, ts, d),
            o_p["k"], o_p["v"], o_s["k"], o_s["v"], st(o_s["cm"]),
            st(o_p["c"]), st(o_p["n"]), st(o_p["m"]), st(o_p["conv"]),
            st(o_s["c"]), st(o_s["n"]), st(o_s["m"]), st(o_s["conv"]),
            st(o_p["ffn"]), st(o_s["ffn"]))
```

```python
import functools

import jax
import jax.numpy as jnp
from jax import lax
from jax.experimental import pallas as pl
from jax.experimental.pallas import tpu as pltpu

F32 = jnp.float32
BF16 = jnp.bfloat16
EPS = 1e-6

VMEM_LIMIT_BYTES = 56 * 1024 * 1024
LANES = 128
SUBLANES = 8

SB_HEAD_DIM = 64
SB_BLOCK = 256
SB_SEG = SB_BLOCK // SUBLANES
SB_HEADS_PER_STEP = 4
SB_PAGES_PER_STEP = 8
PAGE = 128
CM_GROUPS = 8
CM_CHUNK = 128
ML_HEADS = 4
ML_CHUNK = 128
ML_CONV = 4
FFN_CONV = 3
FFN_HALO = 16
FFN_ROWS = 256
ML_HALO = 8


def _cparams(*sem):
    return pltpu.CompilerParams(dimension_semantics=sem, vmem_limit_bytes=VMEM_LIMIT_BYTES)


def _dot(a, b):
    return jnp.dot(a, b, preferred_element_type=F32)


def _dot_nt(a, b):
    return lax.dot_general(a, b, (((1,), (1,)), ((), ())), preferred_element_type=F32)


def _dot_tn(a, b):
    return lax.dot_general(a, b, (((0,), (0,)), ((), ())), preferred_element_type=F32)


def _rms(x, g):
    ms = jnp.mean(x * x, axis=-1, keepdims=True)
    return x * lax.rsqrt(ms + EPS) * g


def _sigmoid(x):
    return 1.0 / (1.0 + jnp.exp(-x))


def _silu(x):
    return x * _sigmoid(x)


def _gelu(x):
    return 0.5 * x * (1.0 + jnp.tanh(0.7978845608028654 * (x + 0.044715 * (x * x * x))))


def _softplus(x):
    return jnp.maximum(x, 0.0) + jnp.log(1.0 + jnp.exp(-jnp.abs(x)))


def _log_sigmoid(x):
    return jnp.minimum(x, 0.0) - jnp.log(1.0 + jnp.exp(-jnp.abs(x)))


def _split3(x):
    hi = x.astype(BF16)
    r1 = x - hi.astype(F32)
    mid = r1.astype(BF16)
    lo = (r1 - mid.astype(F32)).astype(BF16)
    return hi, mid, lo


def _dot_exact_lhs(lhs01, x):
    hi, mid, lo = _split3(x)
    return _dot(lhs01, hi) + _dot(lhs01, mid) + _dot(lhs01, lo)


def _dot_exact_rhs(x, rhs01):
    hi, mid, lo = _split3(x)
    return _dot(hi, rhs01) + _dot(mid, rhs01) + _dot(lo, rhs01)


def _pick_tile(n, candidates):
    for c in candidates:
        if n % c == 0:
            return c
    return n


def _mm_kernel(*refs, norm, act, has_bias, has_res):
    it = iter(refs)
    x_ref = next(it)
    g_ref = next(it) if norm else None
    w_ref = next(it)
    b_ref = next(it) if has_bias else None
    r_ref = next(it) if has_res else None
    o_ref = next(it)
    xn_ref = next(it)

    @pl.when(pl.program_id(1) == 0)
    def _():
        x = x_ref[...].astype(F32)
        if norm:
            x = _rms(x, g_ref[...])
        xn_ref[...] = x.astype(BF16)

    acc = _dot(xn_ref[...], w_ref[...])
    if has_bias:
        acc = acc + b_ref[...]
    if act == "gelu":
        acc = _gelu(acc)
    if has_res:
        acc = acc + r_ref[...]
    o_ref[...] = acc.astype(o_ref.dtype)


def _matmul(x, w, *, g=None, bias=None, res=None, act=None, out_dtype=F32, name="matmul"):
    m, k = x.shape
    n = w.shape[1]
    tm = _pick_tile(m, (512, 256, 128, 16, 8))
    tn = _pick_tile(n, (512, 384, 256, 128))
    norm = g is not None
    args = [x]
    specs = [pl.BlockSpec((tm, k), lambda i, j: (i, 0))]
    if norm:
        args.append(g.reshape(1, k))
        specs.append(pl.BlockSpec((1, k), lambda i, j: (0, 0)))
    args.append(w)
    specs.append(pl.BlockSpec((k, tn), lambda i, j: (0, j)))
    if bias is not None:
        args.append(bias.reshape(1, n))
        specs.append(pl.BlockSpec((1, tn), lambda i, j: (0, j)))
    if res is not None:
        args.append(res)
        specs.append(pl.BlockSpec((tm, tn), lambda i, j: (i, j)))
    kern = functools.partial(_mm_kernel, norm=norm, act=act, has_bias=bias is not None,
                             has_res=res is not None)
    return pl.pallas_call(
        kern,
        grid=(m // tm, n // tn),
        in_specs=specs,
        out_specs=pl.BlockSpec((tm, tn), lambda i, j: (i, j)),
        out_shape=jax.ShapeDtypeStruct((m, n), out_dtype),
        scratch_shapes=[pltpu.VMEM((tm, k), BF16)],
        compiler_params=_cparams("parallel", "arbitrary"),
        name=name,
    )(*args)


def _mm_rows_kernel(*refs, norm, has_res):
    it = iter(refs)
    x_ref = next(it)
    g_ref = next(it) if norm else None
    w_ref = next(it)
    r_ref = next(it) if has_res else None
    o_ref = next(it)
    x = x_ref[...].astype(F32)
    if norm:
        x = _rms(x, g_ref[...])
    acc = _dot(x.astype(BF16), w_ref[...])
    if has_res:
        acc = acc + r_ref[...]
    o_ref[...] = acc.astype(o_ref.dtype)


def _matmul_rows(x, w, *, tm, g=None, res=None, out_dtype=F32, name="matmul_rows"):
    m, k = x.shape
    n = w.shape[1]
    norm = g is not None
    args = [x]
    specs = [pl.BlockSpec((tm, k), lambda i: (i, 0))]
    if norm:
        args.append(g.reshape(1, k))
        specs.append(pl.BlockSpec((1, k), lambda i: (0, 0)))
    args.append(w)
    specs.append(pl.BlockSpec((k, n), lambda i: (0, 0)))
    if res is not None:
        args.append(res)
        specs.append(pl.BlockSpec((tm, n), lambda i: (i, 0)))
    return pl.pallas_call(
        functools.partial(_mm_rows_kernel, norm=norm, has_res=res is not None),
        grid=(m // tm,),
        in_specs=specs,
        out_specs=pl.BlockSpec((tm, n), lambda i: (i, 0)),
        out_shape=jax.ShapeDtypeStruct((m, n), out_dtype),
        compiler_params=_cparams("parallel"),
        name=name,
    )(*args)


def _ffn_seq_kernel(*refs, tiles_per_seq, tm, tf, final):
    (x_ref, halo_ref, g_ref, wg_ref, wu_ref, cwg_ref, cwu_ref, cbg_ref, cbu_ref, wd_ref,
     gp_ref, wgate_ref, p_ref, wproj_ref) = refs[:14]
    gf_ref = refs[14] if final else None
    o_ref, xn_ref, c_ref = refs[15 if final else 14:]
    i = pl.program_id(0)
    j = pl.program_id(1)

    @pl.when(j == 0)
    def _():
        g = g_ref[...]
        xn_ref[FFN_HALO:, :] = _rms(x_ref[...], g).astype(BF16)
        hal = _rms(halo_ref[...], g)
        hal = jnp.where(i % tiles_per_seq == 0, 0.0, hal)
        xn_ref[:FFN_HALO, :] = hal.astype(BF16)

    def conv(xa, w_ref, cw_ref, cb_ref):
        a = _dot(xa, w_ref[...])
        y = cb_ref[...]
        for t in range(FFN_CONV):
            off = FFN_HALO - (FFN_CONV - 1) + t
            y = y + cw_ref[t:t + 1, :] * a[off:off + FFN_ROWS, :]
        return y

    col0 = pl.multiple_of(j * tf, tf)
    for rb in range(tm // FFN_ROWS):
        xa = xn_ref[rb * FFN_ROWS:(rb + 1) * FFN_ROWS + FFN_HALO, :]
        cg = conv(xa, wg_ref, cwg_ref, cbg_ref)
        cu = conv(xa, wu_ref, cwu_ref, cbu_ref)
        c_ref[rb * FFN_ROWS:(rb + 1) * FFN_ROWS, pl.ds(col0, tf)] = (_silu(cg) * cu).astype(BF16)

    @pl.when(j == pl.num_programs(1) - 1)
    def _():
        r = x_ref[...] + _dot(c_ref[...], wd_ref[...])
        gate = _sigmoid(_dot(_rms(r, gp_ref[...]).astype(BF16), wgate_ref[...]))
        out = r + gate * _dot(p_ref[...].astype(BF16), wproj_ref[...])
        if final:
            out = _rms(out, gf_ref[...])
        o_ref[...] = out


def _ffn_ple_seq(r, g, w_up, conv_w, conv_b, w_down, seq_len, g_ple, w_gate, p, w_proj, g_final=None):
    m, d = r.shape
    dff = w_down.shape[0]
    pd = p.shape[1]
    tm = _pick_tile(seq_len, (1024, 512, FFN_ROWS))
    tf = 256
    nj = dff // tf
    hb = tm // FFN_HALO
    final = g_final is not None
    const = lambda i, j: (0, 0)
    once = pl.Buffered(1)
    args = [r, r, g.reshape(1, d), w_up, w_up, conv_w, conv_w, conv_b.reshape(1, -1), conv_b.reshape(1, -1),
            w_down, g_ple.reshape(1, d), w_gate, p, w_proj]
    specs = [
        pl.BlockSpec((tm, d), lambda i, j: (i, 0)),
        pl.BlockSpec((FFN_HALO, d), lambda i, j: (jnp.maximum(i * hb - 1, 0), 0)),
        pl.BlockSpec((1, d), const),
        pl.BlockSpec((d, tf), lambda i, j: (0, j)),
        pl.BlockSpec((d, tf), lambda i, j: (0, nj + j)),
        pl.BlockSpec((FFN_CONV, tf), lambda i, j: (0, j)),
        pl.BlockSpec((FFN_CONV, tf), lambda i, j: (0, nj + j)),
        pl.BlockSpec((1, tf), lambda i, j: (0, j)),
        pl.BlockSpec((1, tf), lambda i, j: (0, nj + j)),
        pl.BlockSpec((dff, d), const, pipeline_mode=once),
        pl.BlockSpec((1, d), const),
        pl.BlockSpec((d, d), const, pipeline_mode=once),
        pl.BlockSpec((tm, pd), lambda i, j: (i, 0)),
        pl.BlockSpec((pd, d), const, pipeline_mode=once),
    ]
    if final:
        args.append(g_final.reshape(1, d))
        specs.append(pl.BlockSpec((1, d), const))
    kern = functools.partial(_ffn_seq_kernel, tiles_per_seq=seq_len // tm, tm=tm, tf=tf, final=final)
    return pl.pallas_call(
        kern,
        grid=(m // tm, nj),
        in_specs=specs,
        out_specs=pl.BlockSpec((tm, d), lambda i, j: (i, 0)),
        out_shape=jax.ShapeDtypeStruct((m, d), F32),
        scratch_shapes=[pltpu.VMEM((tm + FFN_HALO, d), BF16),
                        pltpu.VMEM((tm, dff), BF16)],
        compiler_params=_cparams("parallel", "arbitrary"),
        name="ffn_ple_seq",
    )(*args)


def _ffn_step_kernel(ag_ref, au_ref, b0g_ref, b0u_ref, b1g_ref, b1u_ref, cwg_ref, cwu_ref, cbg_ref, cbu_ref,
                     wd_ref, r_ref, o_ref, acc_ref):
    j = pl.program_id(0)

    @pl.when(j == 0)
    def _():
        acc_ref[...] = jnp.zeros_like(acc_ref)

    def conv(a_ref, b0_ref, b1_ref, cw_ref, cb_ref):
        return (cb_ref[...] + cw_ref[0:1, :] * b0_ref[...] + cw_ref[1:2, :] * b1_ref[...]
                + cw_ref[2:3, :] * a_ref[...])

    cg = conv(ag_ref, b0g_ref, b1g_ref, cwg_ref, cbg_ref)
    cu = conv(au_ref, b0u_ref, b1u_ref, cwu_ref, cbu_ref)
    acc_ref[...] += _dot((_silu(cg) * cu).astype(BF16), wd_ref[...])

    @pl.when(j == pl.num_programs(0) - 1)
    def _():
        o_ref[...] = r_ref[...] + acc_ref[...]


def _ffn_step(r, a, buf0, buf1, conv_w, conv_b, w_down):
    m, d = r.shape
    dff = w_down.shape[0]
    tf = 256
    nj = dff // tf
    lo = lambda j: (0, j)
    hi = lambda j: (0, nj + j)
    cb = conv_b.reshape(1, -1)
    return pl.pallas_call(
        _ffn_step_kernel,
        grid=(nj,),
        in_specs=[
            pl.BlockSpec((m, tf), lo), pl.BlockSpec((m, tf), hi),
            pl.BlockSpec((m, tf), lo), pl.BlockSpec((m, tf), hi),
            pl.BlockSpec((m, tf), lo), pl.BlockSpec((m, tf), hi),
            pl.BlockSpec((FFN_CONV, tf), lo), pl.BlockSpec((FFN_CONV, tf), hi),
            pl.BlockSpec((1, tf), lo), pl.BlockSpec((1, tf), hi),
            pl.BlockSpec((tf, d), lambda j: (j, 0)),
            pl.BlockSpec((m, d), lambda j: (0, 0)),
        ],
        out_specs=pl.BlockSpec((m, d), lambda j: (0, 0)),
        out_shape=jax.ShapeDtypeStruct((m, d), F32),
        scratch_shapes=[pltpu.VMEM((m, d), F32)],
        compiler_params=_cparams("arbitrary"),
        name="ffn_step",
    )(a, a, buf0, buf0, buf1, buf1, conv_w, conv_w, cb, cb, w_down, r)


def _ple_kernel(*refs, final):
    if final:
        r_ref, g_ref, wg_ref, p_ref, wp_ref, gf_ref, o_ref = refs
    else:
        r_ref, g_ref, wg_ref, p_ref, wp_ref, o_ref = refs
    r = r_ref[...]
    xn = _rms(r, g_ref[...]).astype(BF16)
    gate = _sigmoid(_dot(xn, wg_ref[...]))
    pp = _dot(p_ref[...].astype(BF16), wp_ref[...])
    out = r + gate * pp
    if final:
        out = _rms(out, gf_ref[...])
    o_ref[...] = out


def _ple(r, g, w_gate, p, w_proj, g_final=None):
    m, d = r.shape
    pd = p.shape[1]
    tm = _pick_tile(m, (512, 256, 128))
    final = g_final is not None
    args = [r, g.reshape(1, d), w_gate, p, w_proj]
    specs = [
        pl.BlockSpec((tm, d), lambda i: (i, 0)),
        pl.BlockSpec((1, d), lambda i: (0, 0)),
        pl.BlockSpec((d, d), lambda i: (0, 0)),
        pl.BlockSpec((tm, pd), lambda i: (i, 0)),
        pl.BlockSpec((pd, d), lambda i: (0, 0)),
    ]
    if final:
        args.append(g_final.reshape(1, d))
        specs.append(pl.BlockSpec((1, d), lambda i: (0, 0)))
    return pl.pallas_call(
        functools.partial(_ple_kernel, final=final),
        grid=(m // tm,),
        in_specs=specs,
        out_specs=pl.BlockSpec((tm, d), lambda i: (i, 0)),
        out_shape=jax.ShapeDtypeStruct((m, d), F32),
        compiler_params=_cparams("parallel"),
        name="ple",
    )(*args)


def _sb_qkv_seq_kernel(*refs, heads, aliased):
    x_ref, g_ref, wqt_ref, wkt_ref, wvt_ref, wk_ref = refs[:6]
    kt_ref, vt_ref, qt_ref, vtb_ref, kh_ref = refs[6 + (2 if aliased else 0):]
    xn = _rms(x_ref[...], g_ref[...]).astype(BF16)
    qt_ref[...] = _dot_nt(wqt_ref[...], xn).astype(BF16)
    kt_ref[...] = _dot_nt(wkt_ref[...], xn)
    vt = _dot_nt(wvt_ref[...], xn)
    vt_ref[...] = vt
    vtb_ref[...] = vt.astype(BF16)
    k = _dot(xn, wk_ref[...])
    for h in range(heads):
        kh_ref[h] = k[:, h * SB_HEAD_DIM:(h + 1) * SB_HEAD_DIM].astype(BF16)


def _sb_qkv_seq(r, g, w_qkv, layer, n_layers, kt_all, vt_all, seq_len):
    m, d = r.shape
    heads = d // SB_HEAD_DIM
    tm = 256
    tps = seq_len // tm
    wq, wk, wv = w_qkv[:, :d], w_qkv[:, d:2 * d], w_qkv[:, 2 * d:]
    const = lambda i: (0, 0)
    wspec = pl.BlockSpec((d, d), const)
    aliased = kt_all is not None
    args = [r, g.reshape(1, d), wq.T, wk.T, wv.T, wk]
    specs = [pl.BlockSpec((tm, d), lambda i: (i, 0)), pl.BlockSpec((1, d), const), wspec, wspec, wspec, wspec]
    aliases = {}
    if aliased:
        args += [kt_all, vt_all]
        specs += [pl.BlockSpec(memory_space=pl.ANY), pl.BlockSpec(memory_space=pl.ANY)]
        aliases = {6: 0, 7: 1}
    layer_spec = pl.BlockSpec((None, None, d, tm), lambda i: (layer, i // tps, 0, i % tps))
    layer_shape = jax.ShapeDtypeStruct((n_layers, m // seq_len, d, seq_len), F32)
    t_spec = pl.BlockSpec((d, tm), lambda i: (0, i))
    return pl.pallas_call(
        functools.partial(_sb_qkv_seq_kernel, heads=heads, aliased=aliased),
        grid=(m // tm,),
        in_specs=specs,
        out_specs=[layer_spec, layer_spec, t_spec, t_spec,
                   pl.BlockSpec((heads, tm, SB_HEAD_DIM), lambda i: (0, i, 0))],
        out_shape=[layer_shape, layer_shape,
                   jax.ShapeDtypeStruct((d, m), BF16), jax.ShapeDtypeStruct((d, m), BF16),
                   jax.ShapeDtypeStruct((heads, m, SB_HEAD_DIM), BF16)],
        input_output_aliases=aliases,
        compiler_params=_cparams("parallel"),
        name="sb_qkv_seq",
    )(*args)


def _sublane_suffix_exclusive_product(x):
    idx = lax.broadcasted_iota(jnp.int32, x.shape, 0)
    y = jnp.where(idx + 1 < SUBLANES, pltpu.roll(x, SUBLANES - 1, 0), 1.0)
    y = y * jnp.where(idx + 1 < SUBLANES, pltpu.roll(y, SUBLANES - 1, 0), 1.0)
    y = y * jnp.where(idx + 2 < SUBLANES, pltpu.roll(y, SUBLANES - 2, 0), 1.0)
    y = y * jnp.where(idx + 4 < SUBLANES, pltpu.roll(y, SUBLANES - 4, 0), 1.0)
    return y


def _sb_seq_kernel(q_ref, k_ref, v_ref, o_ref, kp_ref, vp_ref, z_ref, e_ref, a_ref, f_ref, acc_ref, c_ref,
                   *, nq, hp):
    tb = SB_BLOCK
    dh = SB_HEAD_DIM
    seg_idx = lax.broadcasted_iota(jnp.int32, (SUBLANES, tb), 0) * SB_SEG
    lane_idx = lax.broadcasted_iota(jnp.int32, (SUBLANES, tb), 1)
    slot = lax.broadcasted_iota(jnp.int32, (tb, tb), 0)
    key = lax.broadcasted_iota(jnp.int32, (tb, tb), 1)
    perm01 = jnp.where(key == (slot & (SUBLANES - 1)) * SB_SEG + (slot >> 3), 1.0, 0.0).astype(BF16)

    for kt in range(nq):
        cols = slice(kt * tb, (kt + 1) * tb)
        for hs in range(hp):
            feat = slice(hs * dh, (hs + 1) * dh)
            kp_ref[hs, cols, :] = _dot(perm01, k_ref[hs, cols, :]).astype(BF16)
            vp_ref[feat, cols] = _dot_nt(v_ref[feat, cols], perm01).astype(BF16)


    def scores(qts, kj):
        k0 = pl.multiple_of(kj * tb, tb)
        return [_dot(kp_ref[hs, pl.ds(k0, tb), :], qts[hs]) for hs in range(hp)]

    def weighted_values(slot, kj):
        k0 = pl.multiple_of(kj * tb, tb)
        return [_dot(vp_ref[hs * dh:(hs + 1) * dh, pl.ds(k0, tb)], a_ref[slot, hs]) for hs in range(hp)]

    def accumulate(slot, parts):
        for hs in range(hp):
            acc_ref[hs * dh:(hs + 1) * dh, :] += parts[hs] * f_ref[slot, hs:hs + 1, :]

    def weights(slot, hs, diag):
        run = jnp.ones((SUBLANES, tb), F32)
        for r in range(SB_SEG - 1, -1, -1):
            rows = slice(r * SUBLANES, (r + 1) * SUBLANES)
            one_minus_beta = 1.0 / (1.0 + jnp.exp(z_ref[slot, hs, rows, :]))
            beta = 1.0 - one_minus_beta
            if diag:
                keep = seg_idx + r < lane_idx
                one_minus_beta = jnp.where(keep, one_minus_beta, 1.0)
                beta = jnp.where(keep, beta, 0.0)
            e_ref[hs, rows, :] = beta * run
            run = run * one_minus_beta
        later = _sublane_suffix_exclusive_product(run)
        for r in range(0, SB_SEG, 2):
            parts = [e_ref[hs, rr * SUBLANES:(rr + 1) * SUBLANES, :] * later for rr in (r, r + 1)]
            a_ref[slot, hs, r * SUBLANES:(r + 2) * SUBLANES, :] = jnp.concatenate(parts, axis=0).astype(BF16)
        return (later * run)[0:1, :]

    def q_body(qi, _):
        q0 = pl.multiple_of(qi * tb, tb)
        qts = [(q_ref[hs * dh:(hs + 1) * dh, pl.ds(q0, tb)].astype(F32) * (dh ** -0.5)).astype(BF16)
               for hs in range(hp)]
        z_first = scores(qts, qi)
        z_next = scores(qts, jnp.maximum(qi - 1, 0))
        acc_ref[...] = jnp.zeros_like(acc_ref)
        for hs in range(hp):
            z_ref[0, hs] = z_first[hs]
            z_ref[1, hs] = z_next[hs]
        for hs in range(hp):
            c_ref[hs:hs + 1, :] = weights(0, hs, True)
            f_ref[0, hs:hs + 1, :] = jnp.ones((1, tb), F32)

        def step(s, slot, prefetch=True):
            prev = 1 - slot
            parts = weighted_values(prev, qi - s + 1)
            if prefetch:
                z_next = scores(qts, jnp.maximum(qi - s - 1, 0))
                for hs in range(hp):
                    z_ref[prev, hs] = z_next[hs]
            for hs in range(hp):
                c_old = c_ref[hs:hs + 1, :]
                f_ref[slot, hs:hs + 1, :] = c_old
                c_ref[hs:hs + 1, :] = c_old * weights(slot, hs, False)
            accumulate(prev, parts)

        def pair_body(p, _):
            step(2 * p + 1, 1)
            step(2 * p + 2, 0)
            return 0

        lax.fori_loop(0, qi // 2, pair_body, 0)

        @pl.when(qi % 2 == 1)
        def _():
            step(qi, 1, prefetch=False)
            accumulate(1, weighted_values(1, 0))

        @pl.when(qi % 2 == 0)
        def _():
            accumulate(0, weighted_values(0, 0))

        o_ref[pl.ds(q0, tb), :] = jnp.transpose(acc_ref[...]).astype(o_ref.dtype)
        return 0

    lax.fori_loop(0, nq, q_body, 0)


def _sb_seq(q_t, k_h, v_t, batch, seq_len):
    d, m = q_t.shape
    heads = d // SB_HEAD_DIM
    hp = SB_HEADS_PER_STEP
    nq = seq_len // SB_BLOCK
    t_spec = pl.BlockSpec((hp * SB_HEAD_DIM, seq_len), lambda b, j: (j, b))
    return pl.pallas_call(
        functools.partial(_sb_seq_kernel, nq=nq, hp=hp),
        grid=(batch, heads // hp),
        in_specs=[t_spec, pl.BlockSpec((hp, seq_len, SB_HEAD_DIM), lambda b, j: (j, b, 0)), t_spec],
        out_specs=pl.BlockSpec((seq_len, hp * SB_HEAD_DIM), lambda b, j: (b, j)),
        out_shape=jax.ShapeDtypeStruct((m, d), BF16),
        scratch_shapes=[pltpu.VMEM((hp, seq_len, SB_HEAD_DIM), BF16),
                        pltpu.VMEM((hp * SB_HEAD_DIM, seq_len), BF16),
                        pltpu.VMEM((2, hp, SB_BLOCK, SB_BLOCK), F32),
                        pltpu.VMEM((hp, SB_BLOCK, SB_BLOCK), F32),
                        pltpu.VMEM((2, hp, SB_BLOCK, SB_BLOCK), BF16),
                        pltpu.VMEM((2, SUBLANES, SB_BLOCK), F32),
                        pltpu.VMEM((hp * SB_HEAD_DIM, SB_BLOCK), F32),
                        pltpu.VMEM((SUBLANES, SB_BLOCK), F32)],
        compiler_params=_cparams("parallel", "parallel"),
        name="sb_seq",
    )(q_t, k_h, v_t)


def _sb_step_kernel(pt_ref, q_ref, *refs, heads, pps):
    k_refs = refs[:pps]
    v_refs = refs[pps:2 * pps]
    o_ref, qb_ref, acc_ref, carry_ref = refs[2 * pps:]
    p = pl.program_id(1)
    d = heads * SB_HEAD_DIM
    own = (lax.broadcasted_iota(jnp.int32, (heads, d), 1) // SB_HEAD_DIM
           == lax.broadcasted_iota(jnp.int32, (heads, d), 0))

    @pl.when(p == 0)
    def _():
        q = jnp.broadcast_to(q_ref[...], (heads, d)) * (SB_HEAD_DIM ** -0.5)
        qb_ref[...] = jnp.where(own, q, 0.0).astype(BF16)
        acc_ref[...] = jnp.zeros_like(acc_ref)
        carry_ref[...] = jnp.zeros_like(carry_ref)

    row = lax.broadcasted_iota(jnp.int32, (PAGE, PAGE), 0)
    col = lax.broadcasted_iota(jnp.int32, (PAGE, PAGE), 1)
    suffix01 = jnp.where(row >= col, 1.0, 0.0).astype(BF16)
    qb = qb_ref[...]
    zs = [_dot(qb, k_refs[u][...].astype(BF16)) for u in range(pps)]
    splits = [_split3(_softplus(z)) for z in zs]
    cums = [_dot(hi, suffix01) + _dot(mid, suffix01) + _dot(lo, suffix01)
            for hi, mid, lo in splits]
    carries = [carry_ref[...]]
    for u in range(pps):
        carries.append(carries[u] + cums[u][:, 0:1])
    ws = [jnp.exp(zs[u] - cums[u] - carries[u]).astype(BF16) for u in range(pps)]
    parts = [_dot_nt(ws[u], v_refs[u][...].astype(BF16)) for u in range(pps)]
    acc = acc_ref[...]
    for part in parts:
        acc = acc + part
    acc_ref[...] = acc
    carry_ref[...] = carries[pps]

    @pl.when(p == pl.num_programs(1) - 1)
    def _():
        o_ref[...] = jnp.sum(jnp.where(own, acc, 0.0), axis=0, keepdims=True)


def _sb_step(q, page_table, cache_k, cache_v, layer):
    b, d = q.shape
    n_pages = page_table.shape[1]
    heads = d // SB_HEAD_DIM
    pps = _pick_tile(n_pages, (SB_PAGES_PER_STEP, 4, 2, 1))

    def page_spec(u):
        return pl.BlockSpec((None, None, d, PAGE),
                            lambda i, p, pt: (layer, pt[i, n_pages - 1 - (p * pps + u)], 0, 0))

    row_spec = pl.BlockSpec((None, 1, d), lambda i, p, pt: (i, 0, 0))
    page_specs = [page_spec(u) for u in range(pps)]
    grid_spec = pltpu.PrefetchScalarGridSpec(
        num_scalar_prefetch=1,
        grid=(b, n_pages // pps),
        in_specs=[row_spec] + page_specs + page_specs,
        out_specs=row_spec,
        scratch_shapes=[pltpu.VMEM((heads, d), BF16), pltpu.VMEM((heads, d), F32),
                        pltpu.VMEM((heads, 1), F32)],
    )
    out = pl.pallas_call(
        functools.partial(_sb_step_kernel, heads=heads, pps=pps),
        grid_spec=grid_spec,
        out_shape=jax.ShapeDtypeStruct((b, 1, d), F32),
        compiler_params=_cparams("parallel", "arbitrary"),
        name="sb_step",
    )(page_table, q.reshape(b, 1, d), *([cache_k] * pps), *([cache_v] * pps))
    return out.reshape(b, d)


def _gmlp_kernel(*refs, single, half, tm):
    if single:
        (x_ref, g_ref, win_ref, bin_ref, lng_ref, lnb_ref, wsv_ref, bsv_ref, wo_ref,
         o_ref, vout_ref, vn_ref, acc_ref) = refs
    else:
        (x_ref, g_ref, win_ref, bin_ref, lng_ref, lnb_ref, ws_ref, bst_ref, wo_ref,
         o_ref, vn_ref, acc_ref) = refs
    gd = half // CM_GROUPS
    pair = 2 * gd
    x = x_ref[...]
    xn = _rms(x, g_ref[...]).astype(BF16)

    v = _gelu(_dot(xn, win_ref[:, half:]) + bin_ref[:, half:])
    mu = jnp.mean(v, axis=-1, keepdims=True)
    var = jnp.mean(jnp.square(v - mu), axis=-1, keepdims=True)
    vn = (v - mu) * lax.rsqrt(var + EPS) * lng_ref[...] + lnb_ref[...]
    if single:
        vout_ref[...] = vn
        vn_ref[...] = vn
    else:
        vn_ref[...] = vn.astype(BF16)
        row = lax.broadcasted_iota(jnp.int32, (CM_CHUNK, CM_CHUNK), 0)
        col = lax.broadcasted_iota(jnp.int32, (CM_CHUNK, CM_CHUNK), 1)
        causal = col <= row

    for gp in range(CM_GROUPS // 2):
        lo = gp * pair
        u = _gelu(_dot(xn, win_ref[:, lo:lo + pair]) + bin_ref[:, lo:lo + pair])
        if single:
            s = vn_ref[:, lo:lo + pair] * wsv_ref[:, lo:lo + pair] + bsv_ref[:, lo:lo + pair]
        else:
            cols = []
            for gi in range(2):
                g = 2 * gp + gi
                ws = jnp.where(causal, ws_ref[g], 0.0).astype(BF16)
                bs = bst_ref[:, g:g + 1]
                rows = []
                for c in range(tm // CM_CHUNK):
                    vc = vn_ref[c * CM_CHUNK:(c + 1) * CM_CHUNK, g * gd:(g + 1) * gd]
                    rows.append(_dot(ws, vc) + bs)
                cols.append(jnp.concatenate(rows, axis=0) if len(rows) > 1 else rows[0])
            s = jnp.concatenate(cols, axis=1)
        y = (u * s).astype(BF16)
        contrib = _dot(y, wo_ref[lo:lo + pair, :])
        if gp == 0:
            acc_ref[...] = contrib
        else:
            acc_ref[...] += contrib
    o_ref[...] = x + acc_ref[...]


def _gmlp(r, g, w_in, b_in, ln_g, ln_b, w_s, b_s, w_o, single):
    m, d = r.shape
    half = w_o.shape[0]
    gd = half // CM_GROUPS
    tm = min(m, 128) if single else 256
    const = lambda i: (0, 0)
    specs = [
        pl.BlockSpec((tm, d), lambda i: (i, 0)),
        pl.BlockSpec((1, d), const),
        pl.BlockSpec((d, 2 * half), const),
        pl.BlockSpec((1, 2 * half), const),
        pl.BlockSpec((1, half), const),
        pl.BlockSpec((1, half), const),
    ]
    args = [r, g.reshape(1, d), w_in, b_in.reshape(1, -1), ln_g.reshape(1, -1), ln_b.reshape(1, -1)]
    if single:
        args += [jnp.repeat(w_s[:, 0, 0], gd).reshape(1, half), jnp.repeat(b_s[:, 0], gd).reshape(1, half)]
        specs += [pl.BlockSpec((1, half), const), pl.BlockSpec((1, half), const)]
    else:
        args += [w_s, jnp.transpose(b_s)]
        specs += [pl.BlockSpec((CM_GROUPS, CM_CHUNK, CM_CHUNK), lambda i: (0, 0, 0)),
                  pl.BlockSpec((CM_CHUNK, CM_GROUPS), const)]
    args.append(w_o)
    specs.append(pl.BlockSpec((half, d), const))
    row_out = pl.BlockSpec((tm, d), lambda i: (i, 0))
    if single:
        out_specs = [row_out, pl.BlockSpec((tm, half), lambda i: (i, 0))]
        out_shape = [jax.ShapeDtypeStruct((m, d), F32), jax.ShapeDtypeStruct((m, half), F32)]
        scratch = [pltpu.VMEM((tm, half), F32), pltpu.VMEM((tm, d), F32)]
    else:
        out_specs = row_out
        out_shape = jax.ShapeDtypeStruct((m, d), F32)
        scratch = [pltpu.VMEM((tm, half), BF16), pltpu.VMEM((tm, d), F32)]
    return pl.pallas_call(
        functools.partial(_gmlp_kernel, single=single, half=half, tm=tm),
        grid=(m // tm,),
        in_specs=specs,
        out_specs=out_specs,
        out_shape=out_shape,
        scratch_shapes=scratch,
        compiler_params=_cparams("parallel"),
        name="gmlp_step" if single else "gmlp_seq",
    )(*args)


def _ml_qkv_tail(xc, x, wq_ref, wk_ref, wv_ref, q_ref, k_ref, v_ref, scale):
    xc = xc.astype(BF16)
    q_ref[...] = _dot(xc, wq_ref[...]).astype(q_ref.dtype)
    k_ref[...] = (_dot(xc, wk_ref[...]) * scale).astype(k_ref.dtype)
    v_ref[...] = _dot(x.astype(BF16), wv_ref[...]).astype(v_ref.dtype)


def _ml_qkv_seq_kernel(x_ref, halo_ref, cw_ref, cb_ref, wq_ref, wk_ref, wv_ref, q_ref, k_ref, v_ref, xs_ref,
                       *, tiles_per_seq, tm, scale):
    i = pl.program_id(1)
    x = x_ref[...]
    xs_ref[:ML_HALO, :] = jnp.where(i % tiles_per_seq == 0, 0.0, halo_ref[...])
    xs_ref[ML_HALO:, :] = x
    y = cb_ref[...]
    for t in range(ML_CONV - 1):
        off = ML_HALO - (ML_CONV - 1) + t
        y = y + cw_ref[t:t + 1, :] * xs_ref[off:off + tm, :]
    y = y + cw_ref[ML_CONV - 1:ML_CONV, :] * x
    _ml_qkv_tail(_silu(y), x, wq_ref, wk_ref, wv_ref, q_ref, k_ref, v_ref, scale)


def _ml_qkv_step_kernel(x_ref, b0_ref, b1_ref, b2_ref, cw_ref, cb_ref, wq_ref, wk_ref, wv_ref,
                        q_ref, k_ref, v_ref, *, scale):
    x = x_ref[...]
    y = (cb_ref[...] + cw_ref[0:1, :] * b0_ref[...] + cw_ref[1:2, :] * b1_ref[...]
         + cw_ref[2:3, :] * b2_ref[...] + cw_ref[3:4, :] * x)
    _ml_qkv_tail(_silu(y), x, wq_ref, wk_ref, wv_ref, q_ref, k_ref, v_ref, scale)


def _ml_qkv(proj, conv_w, conv_b, w_q, w_k, w_v, *, seq_len=None, bufs=None, qv_dtype=BF16):
    m = proj.shape[0]
    nh, hd, _ = w_q.shape
    inner = nh * hd
    scale = hd ** -0.5
    cb = conv_b.reshape(1, inner)
    wspec = pl.BlockSpec((None, hd, hd), lambda h, i: (h, 0, 0))
    cwspec = pl.BlockSpec((ML_CONV, hd), lambda h, i: (0, h))
    cbspec = pl.BlockSpec((1, hd), lambda h, i: (0, h))
    if bufs is None:
        tm = _pick_tile(seq_len, (512, 256, 128))
        hb = tm // ML_HALO
        xspec = pl.BlockSpec((tm, hd), lambda h, i: (i, h))
        kern = functools.partial(_ml_qkv_seq_kernel, tiles_per_seq=seq_len // tm, tm=tm, scale=scale)
        args = [proj, proj, conv_w, cb, w_q, w_k, w_v]
        specs = [xspec, pl.BlockSpec((ML_HALO, hd), lambda h, i: (jnp.maximum(i * hb - 1, 0), h)),
                 cwspec, cbspec, wspec, wspec, wspec]
        scratch = [pltpu.VMEM((tm + ML_HALO, hd), F32)]
    else:
        tm = m
        xspec = pl.BlockSpec((tm, hd), lambda h, i: (i, h))
        kern = functools.partial(_ml_qkv_step_kernel, scale=scale)
        args = [proj, *bufs, conv_w, cb, w_q, w_k, w_v]
        specs = [xspec, xspec, xspec, xspec, cwspec, cbspec, wspec, wspec, wspec]
        scratch = []
    return pl.pallas_call(
        kern,
        grid=(nh, m // tm),
        in_specs=specs,
        out_specs=[xspec, xspec, xspec],
        out_shape=[jax.ShapeDtypeStruct((m, inner), qv_dtype), jax.ShapeDtypeStruct((m, inner), F32),
                   jax.ShapeDtypeStruct((m, inner), qv_dtype)],
        scratch_shapes=scratch,
        compiler_params=_cparams("parallel", "parallel"),
        name="ml_qkv",
    )(*args)


def _ml_in_kernel(x_ref, g_ref, w_ref, wg_ref, bg_ref, o_ref, og_ref):
    xn = _rms(x_ref[...], g_ref[...]).astype(BF16)
    o_ref[...] = _dot(xn, w_ref[...])
    og_ref[...] = _dot(xn, wg_ref[...]) + bg_ref[...]


def _ml_in_seq(x, g, w_main, w_gate, b_gate):
    m, k = x.shape
    n = w_main.shape[1]
    tm = 256
    const = lambda i: (0, 0)
    return pl.pallas_call(
        _ml_in_kernel,
        grid=(m // tm,),
        in_specs=[pl.BlockSpec((tm, k), lambda i: (i, 0)), pl.BlockSpec((1, k), const),
                  pl.BlockSpec((k, n), const), pl.BlockSpec((k, LANES), const), pl.BlockSpec((1, LANES), const)],
        out_specs=[pl.BlockSpec((tm, n), lambda i: (i, 0)), pl.BlockSpec((tm, LANES), lambda i: (i, 0))],
        out_shape=[jax.ShapeDtypeStruct((m, n), F32), jax.ShapeDtypeStruct((m, LANES), F32)],
        compiler_params=_cparams("parallel"),
        name="ml_in",
    )(x, g.reshape(1, k), w_main, w_gate, b_gate.reshape(1, LANES))


def _lane_select(x, idx):
    lane = lax.broadcasted_iota(jnp.int32, x.shape, 1)
    return jnp.sum(jnp.where(lane == idx, x, 0.0), axis=1, keepdims=True)


def _ml_chunk_kernel(q_ref, k_ref, v_ref, gt_ref, h_ref, c_out, n_out, m_out, c_ref, n_ref, m_ref, *, heads):
    c = pl.program_id(1)
    L = ML_CHUNK
    hd = q_ref.shape[1] // heads
    hs = range(heads)
    cols = [slice(j * hd, (j + 1) * hd) for j in hs]

    @pl.when(c == 0)
    def _():
        c_ref[...] = jnp.zeros_like(c_ref)
        n_ref[...] = jnp.zeros_like(n_ref)
        m_ref[...] = jnp.zeros_like(m_ref)

    gt = gt_ref[...]
    row = lax.broadcasted_iota(jnp.int32, (L, L), 0)
    col = lax.broadcasted_iota(jnp.int32, (L, L), 1)
    causal = col <= row
    lower01 = jnp.where(causal, 1.0, 0.0).astype(BF16)
    ones01 = jnp.ones((L, L), BF16)
    i_col = [_lane_select(gt, j) for j in hs]
    f_col = [_log_sigmoid(_lane_select(gt, j + heads)) for j in hs]
    f_b = [jnp.broadcast_to(f_col[j], (L, L)) for j in hs]
    i_b = [jnp.broadcast_to(i_col[j], (L, L)) for j in hs]
    parts_c = [_split3(f_b[j]) for j in hs]
    parts_r = [_split3(jnp.where(row <= col, f_b[j], 0.0)) for j in hs]
    parts_i = [_split3(jnp.where(row == col, i_b[j], 0.0)) for j in hs]
    dot3 = lambda lhs, p: _dot(lhs, p[0]) + _dot(lhs, p[1]) + _dot(lhs, p[2])
    b_c = [dot3(lower01, parts_c[j]) for j in hs]
    b_r = [dot3(ones01, parts_r[j]) for j in hs]
    i_r = [dot3(ones01, parts_i[j]) for j in hs]

    b_col = [b_c[j][:, 0:1] for j in hs]
    m_prev = [m_ref[j, 0:1, 0:1] for j in hs]
    d_log = [jnp.where(causal, b_c[j] - b_r[j] + i_r[j], -jnp.inf) for j in hs]
    inter = [b_col[j] + m_prev[j] for j in hs]
    m_t = [jnp.maximum(inter[j], jnp.max(d_log[j], axis=1, keepdims=True)) for j in hs]
    w_intra = [jnp.exp(d_log[j] - m_t[j]) for j in hs]
    w_inter = [jnp.exp(inter[j] - m_t[j]) for j in hs]

    q = [q_ref[:, cols[j]].astype(BF16) for j in hs]
    k = [k_ref[:, cols[j]] for j in hs]
    kb = [k[j].astype(BF16) for j in hs]
    v = [v_ref[:, cols[j]].astype(BF16) for j in hs]
    c_old = [c_ref[j] for j in hs]
    n_old = [n_ref[j] for j in hs]
    s = [_dot_nt(q[j], kb[j]) for j in hs]
    q_c = [_dot(q[j], c_old[j].astype(BF16)) for j in hs]
    q_n = [_dot_nt(q[j], n_old[j].astype(BF16))[:, 0:1] for j in hs]
    qk = [s[j] * w_intra[j] for j in hs]
    qk_v = [_dot(qk[j].astype(BF16), v[j]) for j in hs]
    for j in hs:
        num = w_inter[j] * q_c[j] + qk_v[j]
        den = w_inter[j] * q_n[j] + jnp.sum(qk[j], axis=1, keepdims=True)
        h_ref[:, cols[j]] = num / jnp.maximum(jnp.abs(den), jnp.exp(-m_t[j]))

    m_new = [m_t[j][L - 1:L, :] for j in hs]
    b_last = [b_col[j][L - 1:L, :] for j in hs]
    w_state = [jnp.exp(b_last[j] + m_prev[j] - m_new[j]) for j in hs]
    w_rows_c = [jnp.exp(b_last[j] - b_col[j] + i_col[j] - m_new[j]) for j in hs]
    w_rows_r = [jnp.exp(b_last[j] - b_r[j][0:SUBLANES, :] + i_r[j][0:SUBLANES, :] - m_new[j])
                for j in hs]
    kw = [(k[j] * w_rows_c[j]).astype(BF16) for j in hs]
    c_upd = [_dot_tn(kw[j], v[j]) for j in hs]
    n_upd = [_dot(w_rows_r[j].astype(BF16), kb[j]) for j in hs]
    for j in hs:
        c_ref[j] = w_state[j] * c_old[j] + c_upd[j]
        n_ref[j] = w_state[j] * n_old[j] + n_upd[j]
        m_ref[j] = jnp.broadcast_to(m_new[j], m_ref.shape[1:])

    @pl.when(c == pl.num_programs(1) - 1)
    def _():
        c_out[...] = c_ref[...]
        for j in hs:
            n_out[j] = n_ref[j, 0:1, :]
            m_out[j] = m_ref[j, 0:1, :]


def _ml_chunks(q, k, v, gates, batch, seq_len, heads):
    m, inner = q.shape
    hd = inner // heads
    nc = seq_len // ML_CHUNK
    xspec = pl.BlockSpec((ML_CHUNK, inner), lambda b, c: (b * nc + c, 0))
    return pl.pallas_call(
        functools.partial(_ml_chunk_kernel, heads=heads),
        grid=(batch, nc),
        in_specs=[xspec, xspec, xspec, pl.BlockSpec((ML_CHUNK, LANES), lambda b, c: (b * nc + c, 0))],
        out_specs=[xspec,
                   pl.BlockSpec((None, heads, hd, hd), lambda b, c: (b, 0, 0, 0)),
                   pl.BlockSpec((None, heads, 1, hd), lambda b, c: (b, 0, 0, 0)),
                   pl.BlockSpec((None, heads, 1, LANES), lambda b, c: (b, 0, 0, 0))],
        out_shape=[jax.ShapeDtypeStruct((m, inner), F32),
                   jax.ShapeDtypeStruct((batch, heads, hd, hd), F32),
                   jax.ShapeDtypeStruct((batch, heads, 1, hd), F32),
                   jax.ShapeDtypeStruct((batch, heads, 1, LANES), F32)],
        scratch_shapes=[pltpu.VMEM((heads, hd, hd), F32), pltpu.VMEM((heads, SUBLANES, hd), F32),
                        pltpu.VMEM((heads, SUBLANES, LANES), F32)],
        compiler_params=_cparams("parallel", "arbitrary"),
        name="ml_chunks",
    )(q, k, v, gates)


def _ml_step_kernel(q_ref, k_ref, v_ref, gt_ref, mp_ref, c_ref, n_ref, h_ref, c_out, n_out, m_out, *, heads):
    gt = gt_ref[...]
    mp = mp_ref[...]
    hd = q_ref.shape[1] // heads
    lane = lax.broadcasted_iota(jnp.int32, (1, LANES), 1)
    first_row = lax.broadcasted_iota(jnp.int32, (SUBLANES, hd), 0) == 0
    m_all = jnp.zeros((1, LANES), F32)
    for j in range(heads):
        cols = slice(j * hd, (j + 1) * hd)
        i_g = _lane_select(gt, j)
        f_g = _log_sigmoid(_lane_select(gt, j + heads))
        inter = f_g + _lane_select(mp, j)
        m_t = jnp.maximum(inter, i_g)
        w_in = jnp.exp(i_g - m_t)
        w_st = jnp.exp(inter - m_t)

        q = q_ref[:, cols]
        k = k_ref[:, cols]
        v = v_ref[:, cols]
        n = n_ref[:, cols]
        c_old = c_ref[j]
        q8 = jnp.broadcast_to(q, (SUBLANES, hd)).astype(BF16)
        q_c = _dot(q8, c_old.astype(BF16))[0:1, :]
        qk = jnp.sum(q * k, axis=1, keepdims=True) * w_in
        num = w_st * q_c + qk * v
        den = w_st * jnp.sum(q * n, axis=1, keepdims=True) + qk
        h_ref[:, cols] = num / jnp.maximum(jnp.abs(den), jnp.exp(-m_t))

        kw = k * w_in
        kw8 = jnp.where(first_row, jnp.broadcast_to(kw, (SUBLANES, hd)), 0.0).astype(BF16)
        v8 = jnp.broadcast_to(v, (SUBLANES, hd)).astype(BF16)
        c_out[j] = w_st * c_old + _dot_tn(kw8, v8)
        n_out[:, cols] = w_st * n + kw
        m_all = jnp.where(lane == j, m_t, m_all)
    m_out[...] = m_all


def _ml_step(q, k, v, gates, m_prev, c_state, n_state, heads):
    b, inner = q.shape
    hd = inner // heads
    r3 = lambda a: a.reshape(b, 1, -1)
    vspec = pl.BlockSpec((None, 1, inner), lambda i: (i, 0, 0))
    gspec = pl.BlockSpec((None, 1, LANES), lambda i: (i, 0, 0))
    cspec = pl.BlockSpec((None, heads, hd, hd), lambda i: (i, 0, 0, 0))
    return pl.pallas_call(
        functools.partial(_ml_step_kernel, heads=heads),
        grid=(b,),
        in_specs=[vspec, vspec, vspec, gspec, gspec, cspec, vspec],
        out_specs=[vspec, cspec, vspec, gspec],
        out_shape=[jax.ShapeDtypeStruct((b, 1, inner), F32),
                   jax.ShapeDtypeStruct((b, heads, hd, hd), F32),
                   jax.ShapeDtypeStruct((b, 1, inner), F32),
                   jax.ShapeDtypeStruct((b, 1, LANES), F32)],
        compiler_params=_cparams("parallel"),
        name="ml_step",
    )(r3(q), r3(k), r3(v), r3(gates), r3(m_prev), c_state, r3(n_state))


def _ml_out_kernel(h_ref, o_ref, ng_ref, wo_ref, r_ref, out_ref, *, heads):
    h = h_ref[...]
    hd = h.shape[1] // heads
    parts = []
    for j in range(heads):
        hh = h[:, j * hd:(j + 1) * hd]
        mu = jnp.mean(hh, axis=-1, keepdims=True)
        var = jnp.mean(jnp.square(hh - mu), axis=-1, keepdims=True)
        parts.append((hh - mu) * lax.rsqrt(var + EPS))
    hn = jnp.concatenate(parts, axis=1) * ng_ref[...]
    out = (_sigmoid(o_ref[...]) * hn).astype(BF16)
    out_ref[...] = r_ref[...] + _dot(out, wo_ref[...])


def _ml_out(h, proj, norm_g, w_o, r, heads):
    m, inner = h.shape
    d = r.shape[1]
    tm = _pick_tile(m, (256, 128))
    return pl.pallas_call(
        functools.partial(_ml_out_kernel, heads=heads),
        grid=(m // tm,),
        in_specs=[pl.BlockSpec((tm, inner), lambda i: (i, 0)),
                  pl.BlockSpec((tm, inner), lambda i: (i, 1)),
                  pl.BlockSpec((1, inner), lambda i: (0, 0)),
                  pl.BlockSpec((inner, d), lambda i: (0, 0)),
                  pl.BlockSpec((tm, d), lambda i: (i, 0))],
        out_specs=pl.BlockSpec((tm, d), lambda i: (i, 0)),
        out_shape=jax.ShapeDtypeStruct((m, d), F32),
        compiler_params=_cparams("parallel"),
        name="ml_out",
    )(h, proj, norm_g.reshape(1, inner), w_o, r)


def _trunk(x, p, seq_len, weights, *, past=None, ml_state=None, ffn_state=None):
    (norm_mix, norm_ffn, norm_ple, norm_final, sb_w_qkv, sb_w_o, cm_w_in, cm_b_in, cm_ln_g, cm_ln_b,
     cm_w_s, cm_b_s, cm_w_o, ml_w_in, ml_b_gates, ml_conv_w, ml_conv_b, ml_w_q, ml_w_k, ml_w_v,
     ml_norm_g, ml_w_o, ffn_w_up, ffn_conv_w, ffn_conv_b, ffn_w_down, ple_w_proj, ple_w_gate) = weights
    m, d = x.shape
    batch = m // seq_len
    depth = norm_mix.shape[0]
    n_sb = sb_w_qkv.shape[0]
    step = seq_len == 1
    heads = d // SB_HEAD_DIM
    outs = dict(k=[], v=[], cm=[], c=[], n=[], m=[], conv=[], ffn=[])
    kt_all = vt_all = None
    r = x
    for i in range(depth):
        kind, j = i % 3, i // 3
        if kind == 0:
            if step:
                qkv = _matmul(r, sb_w_qkv[j], g=norm_mix[i], name="sb_qkv")
                q, k, v = qkv[:, :d], qkv[:, d:2 * d], qkv[:, 2 * d:]
                outs["k"].append(k.reshape(batch, seq_len, heads, SB_HEAD_DIM))
                outs["v"].append(v.reshape(batch, seq_len, heads, SB_HEAD_DIM))
                page_table, cache_k, cache_v = past
                o = _sb_step(q, page_table, cache_k, cache_v, j)
            else:
                kt_all, vt_all, q_t, v_t, k_h = _sb_qkv_seq(r, norm_mix[i], sb_w_qkv[j], j, n_sb, kt_all, vt_all,
                                                           seq_len)
                o = _sb_seq(q_t, k_h, v_t, batch, seq_len)
            if step:
                r = _matmul(o, sb_w_o[j], res=r, name="sb_out")
            else:
                r = _matmul_rows(o, sb_w_o[j], tm=512, res=r, name="sb_out")
        elif kind == 1:
            res = _gmlp(r, norm_mix[i], cm_w_in[j], cm_b_in[j], cm_ln_g[j], cm_ln_b[j], cm_w_s[j], cm_b_s[j],
                        cm_w_o[j], step)
            if step:
                r, vn = res
                outs["cm"].append(vn.reshape(batch, 1, -1))
            else:
                r = res
        else:
            nh = ML_HEADS
            inner = ml_w_q.shape[2] * nh
            w_main = ml_w_in[j][:, :2 * inner]
            w_gate = jnp.pad(ml_w_in[j][:, 2 * inner:], ((0, 0), (0, LANES - 2 * nh)))
            b_gate = jnp.pad(ml_b_gates[j], (0, LANES - 2 * nh))
            if step:
                proj = _matmul(r, w_main, g=norm_mix[i], name="ml_in")
                gates = _matmul(r, w_gate, g=norm_mix[i], bias=b_gate, name="ml_gates")
            else:
                proj, gates = _ml_in_seq(r, norm_mix[i], w_main, w_gate, b_gate)
            if step:
                c0, n0, m0, conv0 = ml_state
                bufs = [conv0[j][:, t] for t in range(ML_CONV - 1)]
                q, k, v = _ml_qkv(proj, ml_conv_w[j], ml_conv_b[j], ml_w_q[j], ml_w_k[j], ml_w_v[j],
                                  bufs=bufs, qv_dtype=F32)
                m_prev = jnp.pad(m0[j], ((0, 0), (0, LANES - nh)))
                hh, c_new, n_new, m_new = _ml_step(q, k, v, gates, m_prev, c0[j], n0[j], nh)
                hh = hh.reshape(m, inner)
                m_new = m_new[:, 0, :nh]
                outs["conv"].append(jnp.concatenate([conv0[j][:, 1:], proj[:, None, :inner]], axis=1))
            else:
                q, k, v = _ml_qkv(proj, ml_conv_w[j], ml_conv_b[j], ml_w_q[j], ml_w_k[j], ml_w_v[j],
                                  seq_len=seq_len)
                hh, c_new, n_new, m_new = _ml_chunks(q, k, v, gates, batch, seq_len, nh)
                m_new = m_new[:, :, 0, 0]
                tail = proj.reshape(batch, seq_len, -1)[:, seq_len - (ML_CONV - 1):, :inner]
                outs["conv"].append(tail)
            outs["c"].append(c_new)
            outs["n"].append(n_new.reshape(batch, nh, -1))
            outs["m"].append(m_new)
            r = _ml_out(hh, proj, ml_norm_g[j], ml_w_o[j], r, nh)

        g_final = norm_final if i == depth - 1 else None
        if step:
            a = _matmul(r, ffn_w_up[i], g=norm_ffn[i], name="ffn_up")
            buf = ffn_state[i]
            r = _ffn_step(r, a, buf[:, 0], buf[:, 1], ffn_conv_w[i], ffn_conv_b[i], ffn_w_down[i])
            outs["ffn"].append(jnp.concatenate([buf[:, 1:], a[:, None, :]], axis=1))
            r = _ple(r, norm_ple[i], ple_w_gate[i], p[i], ple_w_proj[i], g_final=g_final)
        else:
            tail = r.reshape(batch, seq_len, d)[:, seq_len - (FFN_CONV - 1):].reshape(-1, d)
            a_tail = _matmul(tail, ffn_w_up[i], g=norm_ffn[i], name="ffn_tail")
            outs["ffn"].append(a_tail.reshape(batch, FFN_CONV - 1, -1))
            r = _ffn_ple_seq(r, norm_ffn[i], ffn_w_up[i], ffn_conv_w[i], ffn_conv_b[i], ffn_w_down[i], seq_len,
                             norm_ple[i], ple_w_gate[i], p[i], ple_w_proj[i], g_final=g_final)
    if not step:
        rows = lambda a: a.reshape(n_sb, batch, heads, SB_HEAD_DIM, seq_len).transpose(0, 1, 4, 2, 3)
        outs["k"], outs["v"] = rows(kt_all), rows(vt_all)
    else:
        outs["k"], outs["v"] = jnp.stack(outs["k"]), jnp.stack(outs["v"])
    return r, outs


def kernel(x_prompt, x_sample, p_prompt, p_sample, page_table, cache_k, cache_v, state_mlstm_c, state_mlstm_n, state_mlstm_m, state_mlstm_conv, state_ffn_conv, norm_mix, norm_ffn, norm_ple, norm_final, sb_w_qkv, sb_w_o, cm_w_in, cm_b_in, cm_ln_g, cm_ln_b, cm_w_s, cm_b_s, cm_w_o, ml_w_in, ml_b_gates, ml_conv_w, ml_conv_b, ml_w_q, ml_w_k, ml_w_v, ml_norm_g, ml_w_o, ffn_w_up, ffn_conv_w, ffn_conv_b, ffn_w_down, ple_w_proj, ple_w_gate):
    bf = lambda w: w.astype(BF16)
    weights = (norm_mix, norm_ffn, norm_ple, norm_final, bf(sb_w_qkv), bf(sb_w_o), bf(cm_w_in), cm_b_in,
               cm_ln_g, cm_ln_b, cm_w_s, cm_b_s, bf(cm_w_o), bf(ml_w_in), ml_b_gates, ml_conv_w, ml_conv_b,
               bf(ml_w_q), bf(ml_w_k), bf(ml_w_v), ml_norm_g, bf(ml_w_o), bf(ffn_w_up), ffn_conv_w,
               ffn_conv_b, bf(ffn_w_down), bf(ple_w_proj), bf(ple_w_gate))
    b, t, d = x_prompt.shape
    bs, ts, _ = x_sample.shape
    depth = p_prompt.shape[0]
    n_sb, n_phys, page, sbh, sbd = cache_k.shape
    pages = lambda c: c.transpose(0, 1, 3, 4, 2).reshape(n_sb, n_phys, sbh * sbd, page)

    y_p, o_p = _trunk(x_prompt.reshape(b * t, d), p_prompt.reshape(depth, b * t, -1), t, weights)
    y_s, o_s = _trunk(
        x_sample.reshape(bs * ts, d), p_sample.reshape(depth, bs * ts, -1), ts, weights,
        past=(page_table, pages(cache_k), pages(cache_v)),
        ml_state=(state_mlstm_c, state_mlstm_n, state_mlstm_m, state_mlstm_conv),
        ffn_state=state_ffn_conv)

    st = jnp.stack
    return (y_p.reshape(b, t, d), y_s.reshape(bs, ts, d),
            o_p["k"], o_p["v"], o_s["k"], o_s["v"], st(o_s["cm"]),
            st(o_p["c"]), st(o_p["n"]), st(o_p["m"]), st(o_p["conv"]),
            st(o_s["c"]), st(o_s["n"]), st(o_s["m"]), st(o_s["conv"]),
            st(o_p["ffn"]), st(o_s["ffn"]))
```

```python
import functools

import jax
import jax.numpy as jnp
from jax import lax
from jax.experimental import pallas as pl
from jax.experimental.pallas import tpu as pltpu

F32 = jnp.float32
BF16 = jnp.bfloat16
EPS = 1e-6

VMEM_LIMIT_BYTES = 56 * 1024 * 1024
LANES = 128
SUBLANES = 8

SB_HEAD_DIM = 64
SB_BLOCK = 256
SB_SEG = SB_BLOCK // SUBLANES
SB_HEADS_PER_STEP = 4
SB_PAGES_PER_STEP = 8
PAGE = 128
CM_GROUPS = 8
CM_CHUNK = 128
ML_HEADS = 4
ML_CHUNK = 128
ML_CONV = 4
FFN_CONV = 3
FFN_HALO = 16
ML_HALO = 8


def _cparams(*sem):
    return pltpu.CompilerParams(dimension_semantics=sem, vmem_limit_bytes=VMEM_LIMIT_BYTES)


def _dot(a, b):
    return jnp.dot(a, b, preferred_element_type=F32)


def _dot_nt(a, b):
    return lax.dot_general(a, b, (((1,), (1,)), ((), ())), preferred_element_type=F32)


def _dot_tn(a, b):
    return lax.dot_general(a, b, (((0,), (0,)), ((), ())), preferred_element_type=F32)


def _rms(x, g):
    ms = jnp.mean(x * x, axis=-1, keepdims=True)
    return x * lax.rsqrt(ms + EPS) * g


def _sigmoid(x):
    return 1.0 / (1.0 + jnp.exp(-x))


def _silu(x):
    return x * _sigmoid(x)


def _gelu(x):
    return 0.5 * x * (1.0 + jnp.tanh(0.7978845608028654 * (x + 0.044715 * (x * x * x))))


def _softplus(x):
    return jnp.maximum(x, 0.0) + jnp.log(1.0 + jnp.exp(-jnp.abs(x)))


def _log_sigmoid(x):
    return jnp.minimum(x, 0.0) - jnp.log(1.0 + jnp.exp(-jnp.abs(x)))


def _split3(x):
    hi = x.astype(BF16)
    r1 = x - hi.astype(F32)
    mid = r1.astype(BF16)
    lo = (r1 - mid.astype(F32)).astype(BF16)
    return hi, mid, lo


def _dot_exact_lhs(lhs01, x):
    hi, mid, lo = _split3(x)
    return _dot(lhs01, hi) + _dot(lhs01, mid) + _dot(lhs01, lo)


def _dot_exact_rhs(x, rhs01):
    hi, mid, lo = _split3(x)
    return _dot(hi, rhs01) + _dot(mid, rhs01) + _dot(lo, rhs01)


def _pick_tile(n, candidates):
    for c in candidates:
        if n % c == 0:
            return c
    return n


def _mm_kernel(*refs, norm, act, has_bias, has_res):
    it = iter(refs)
    x_ref = next(it)
    g_ref = next(it) if norm else None
    w_ref = next(it)
    b_ref = next(it) if has_bias else None
    r_ref = next(it) if has_res else None
    o_ref = next(it)
    xn_ref = next(it)

    @pl.when(pl.program_id(1) == 0)
    def _():
        x = x_ref[...].astype(F32)
        if norm:
            x = _rms(x, g_ref[...])
        xn_ref[...] = x.astype(BF16)

    acc = _dot(xn_ref[...], w_ref[...])
    if has_bias:
        acc = acc + b_ref[...]
    if act == "gelu":
        acc = _gelu(acc)
    if has_res:
        acc = acc + r_ref[...]
    o_ref[...] = acc.astype(o_ref.dtype)


def _matmul(x, w, *, g=None, bias=None, res=None, act=None, out_dtype=F32, name="matmul"):
    m, k = x.shape
    n = w.shape[1]
    tm = _pick_tile(m, (512, 256, 128, 16, 8))
    tn = _pick_tile(n, (512, 384, 256, 128))
    norm = g is not None
    args = [x]
    specs = [pl.BlockSpec((tm, k), lambda i, j: (i, 0))]
    if norm:
        args.append(g.reshape(1, k))
        specs.append(pl.BlockSpec((1, k), lambda i, j: (0, 0)))
    args.append(w)
    specs.append(pl.BlockSpec((k, tn), lambda i, j: (0, j)))
    if bias is not None:
        args.append(bias.reshape(1, n))
        specs.append(pl.BlockSpec((1, tn), lambda i, j: (0, j)))
    if res is not None:
        args.append(res)
        specs.append(pl.BlockSpec((tm, tn), lambda i, j: (i, j)))
    kern = functools.partial(_mm_kernel, norm=norm, act=act, has_bias=bias is not None,
                             has_res=res is not None)
    return pl.pallas_call(
        kern,
        grid=(m // tm, n // tn),
        in_specs=specs,
        out_specs=pl.BlockSpec((tm, tn), lambda i, j: (i, j)),
        out_shape=jax.ShapeDtypeStruct((m, n), out_dtype),
        scratch_shapes=[pltpu.VMEM((tm, k), BF16)],
        compiler_params=_cparams("parallel", "arbitrary"),
        name=name,
    )(*args)


def _mm_rows_kernel(*refs, norm, has_res):
    it = iter(refs)
    x_ref = next(it)
    g_ref = next(it) if norm else None
    w_ref = next(it)
    r_ref = next(it) if has_res else None
    o_ref = next(it)
    x = x_ref[...].astype(F32)
    if norm:
        x = _rms(x, g_ref[...])
    acc = _dot(x.astype(BF16), w_ref[...])
    if has_res:
        acc = acc + r_ref[...]
    o_ref[...] = acc.astype(o_ref.dtype)


def _matmul_rows(x, w, *, tm, g=None, res=None, out_dtype=F32, name="matmul_rows"):
    m, k = x.shape
    n = w.shape[1]
    norm = g is not None
    args = [x]
    specs = [pl.BlockSpec((tm, k), lambda i: (i, 0))]
    if norm:
        args.append(g.reshape(1, k))
        specs.append(pl.BlockSpec((1, k), lambda i: (0, 0)))
    args.append(w)
    specs.append(pl.BlockSpec((k, n), lambda i: (0, 0)))
    if res is not None:
        args.append(res)
        specs.append(pl.BlockSpec((tm, n), lambda i: (i, 0)))
    return pl.pallas_call(
        functools.partial(_mm_rows_kernel, norm=norm, has_res=res is not None),
        grid=(m // tm,),
        in_specs=specs,
        out_specs=pl.BlockSpec((tm, n), lambda i: (i, 0)),
        out_shape=jax.ShapeDtypeStruct((m, n), out_dtype),
        compiler_params=_cparams("parallel"),
        name=name,
    )(*args)


def _ffn_seq_kernel(*refs, tiles_per_seq, tm, tf, final):
    (x_ref, halo_ref, g_ref, wg_ref, wu_ref, cwg_ref, cwu_ref, cbg_ref, cbu_ref, wd_ref,
     gp_ref, wgate_ref, p_ref, wproj_ref) = refs[:14]
    gf_ref = refs[14] if final else None
    o_ref, xn_ref, a_ref, c_ref = refs[15 if final else 14:]
    i = pl.program_id(0)
    j = pl.program_id(1)

    @pl.when(j == 0)
    def _():
        g = g_ref[...]
        xn_ref[FFN_HALO:, :] = _rms(x_ref[...], g).astype(BF16)
        hal = _rms(halo_ref[...], g)
        hal = jnp.where(i % tiles_per_seq == 0, 0.0, hal)
        xn_ref[:FFN_HALO, :] = hal.astype(BF16)

    xa = xn_ref[...]
    a_ref[:, :tf] = _dot(xa, wg_ref[...])
    a_ref[:, tf:] = _dot(xa, wu_ref[...])

    def conv(lo, cw_ref, cb_ref):
        y = cb_ref[...]
        for t in range(FFN_CONV):
            off = FFN_HALO - (FFN_CONV - 1) + t
            y = y + cw_ref[t:t + 1, :] * a_ref[off:off + tm, lo:lo + tf]
        return y

    cg = conv(0, cwg_ref, cbg_ref)
    cu = conv(tf, cwu_ref, cbu_ref)
    c_ref[:, pl.ds(pl.multiple_of(j * tf, tf), tf)] = (_silu(cg) * cu).astype(BF16)

    @pl.when(j == pl.num_programs(1) - 1)
    def _():
        r = x_ref[...] + _dot(c_ref[...], wd_ref[...])
        gate = _sigmoid(_dot(_rms(r, gp_ref[...]).astype(BF16), wgate_ref[...]))
        out = r + gate * _dot(p_ref[...].astype(BF16), wproj_ref[...])
        if final:
            out = _rms(out, gf_ref[...])
        o_ref[...] = out


def _ffn_ple_seq(r, g, w_up, conv_w, conv_b, w_down, seq_len, g_ple, w_gate, p, w_proj, g_final=None):
    m, d = r.shape
    dff = w_down.shape[0]
    pd = p.shape[1]
    tm = _pick_tile(seq_len, (1024, 512, 256, 128))
    tf = 256
    nj = dff // tf
    hb = tm // FFN_HALO
    final = g_final is not None
    const = lambda i, j: (0, 0)
    once = pl.Buffered(1)
    args = [r, r, g.reshape(1, d), w_up, w_up, conv_w, conv_w, conv_b.reshape(1, -1), conv_b.reshape(1, -1),
            w_down, g_ple.reshape(1, d), w_gate, p, w_proj]
    specs = [
        pl.BlockSpec((tm, d), lambda i, j: (i, 0)),
        pl.BlockSpec((FFN_HALO, d), lambda i, j: (jnp.maximum(i * hb - 1, 0), 0)),
        pl.BlockSpec((1, d), const),
        pl.BlockSpec((d, tf), lambda i, j: (0, j)),
        pl.BlockSpec((d, tf), lambda i, j: (0, nj + j)),
        pl.BlockSpec((FFN_CONV, tf), lambda i, j: (0, j)),
        pl.BlockSpec((FFN_CONV, tf), lambda i, j: (0, nj + j)),
        pl.BlockSpec((1, tf), lambda i, j: (0, j)),
        pl.BlockSpec((1, tf), lambda i, j: (0, nj + j)),
        pl.BlockSpec((dff, d), const, pipeline_mode=once),
        pl.BlockSpec((1, d), const),
        pl.BlockSpec((d, d), const, pipeline_mode=once),
        pl.BlockSpec((tm, pd), lambda i, j: (i, 0)),
        pl.BlockSpec((pd, d), const, pipeline_mode=once),
    ]
    if final:
        args.append(g_final.reshape(1, d))
        specs.append(pl.BlockSpec((1, d), const))
    kern = functools.partial(_ffn_seq_kernel, tiles_per_seq=seq_len // tm, tm=tm, tf=tf, final=final)
    return pl.pallas_call(
        kern,
        grid=(m // tm, nj),
        in_specs=specs,
        out_specs=pl.BlockSpec((tm, d), lambda i, j: (i, 0)),
        out_shape=jax.ShapeDtypeStruct((m, d), F32),
        scratch_shapes=[pltpu.VMEM((tm + FFN_HALO, d), BF16),
                        pltpu.VMEM((tm + FFN_HALO, 2 * tf), F32),
                        pltpu.VMEM((tm, dff), BF16)],
        compiler_params=_cparams("parallel", "arbitrary"),
        name="ffn_ple_seq",
    )(*args)


def _ffn_step_kernel(ag_ref, au_ref, b0g_ref, b0u_ref, b1g_ref, b1u_ref, cwg_ref, cwu_ref, cbg_ref, cbu_ref,
                     wd_ref, r_ref, o_ref, acc_ref):
    j = pl.program_id(0)

    @pl.when(j == 0)
    def _():
        acc_ref[...] = jnp.zeros_like(acc_ref)

    def conv(a_ref, b0_ref, b1_ref, cw_ref, cb_ref):
        return (cb_ref[...] + cw_ref[0:1, :] * b0_ref[...] + cw_ref[1:2, :] * b1_ref[...]
                + cw_ref[2:3, :] * a_ref[...])

    cg = conv(ag_ref, b0g_ref, b1g_ref, cwg_ref, cbg_ref)
    cu = conv(au_ref, b0u_ref, b1u_ref, cwu_ref, cbu_ref)
    acc_ref[...] += _dot((_silu(cg) * cu).astype(BF16), wd_ref[...])

    @pl.when(j == pl.num_programs(0) - 1)
    def _():
        o_ref[...] = r_ref[...] + acc_ref[...]


def _ffn_step(r, a, buf0, buf1, conv_w, conv_b, w_down):
    m, d = r.shape
    dff = w_down.shape[0]
    tf = 256
    nj = dff // tf
    lo = lambda j: (0, j)
    hi = lambda j: (0, nj + j)
    cb = conv_b.reshape(1, -1)
    return pl.pallas_call(
        _ffn_step_kernel,
        grid=(nj,),
        in_specs=[
            pl.BlockSpec((m, tf), lo), pl.BlockSpec((m, tf), hi),
            pl.BlockSpec((m, tf), lo), pl.BlockSpec((m, tf), hi),
            pl.BlockSpec((m, tf), lo), pl.BlockSpec((m, tf), hi),
            pl.BlockSpec((FFN_CONV, tf), lo), pl.BlockSpec((FFN_CONV, tf), hi),
            pl.BlockSpec((1, tf), lo), pl.BlockSpec((1, tf), hi),
            pl.BlockSpec((tf, d), lambda j: (j, 0)),
            pl.BlockSpec((m, d), lambda j: (0, 0)),
        ],
        out_specs=pl.BlockSpec((m, d), lambda j: (0, 0)),
        out_shape=jax.ShapeDtypeStruct((m, d), F32),
        scratch_shapes=[pltpu.VMEM((m, d), F32)],
        compiler_params=_cparams("arbitrary"),
        name="ffn_step",
    )(a, a, buf0, buf0, buf1, buf1, conv_w, conv_w, cb, cb, w_down, r)


def _ple_kernel(*refs, final):
    if final:
        r_ref, g_ref, wg_ref, p_ref, wp_ref, gf_ref, o_ref = refs
    else:
        r_ref, g_ref, wg_ref, p_ref, wp_ref, o_ref = refs
    r = r_ref[...]
    xn = _rms(r, g_ref[...]).astype(BF16)
    gate = _sigmoid(_dot(xn, wg_ref[...]))
    pp = _dot(p_ref[...].astype(BF16), wp_ref[...])
    out = r + gate * pp
    if final:
        out = _rms(out, gf_ref[...])
    o_ref[...] = out


def _ple(r, g, w_gate, p, w_proj, g_final=None):
    m, d = r.shape
    pd = p.shape[1]
    tm = _pick_tile(m, (512, 256, 128))
    final = g_final is not None
    args = [r, g.reshape(1, d), w_gate, p, w_proj]
    specs = [
        pl.BlockSpec((tm, d), lambda i: (i, 0)),
        pl.BlockSpec((1, d), lambda i: (0, 0)),
        pl.BlockSpec((d, d), lambda i: (0, 0)),
        pl.BlockSpec((tm, pd), lambda i: (i, 0)),
        pl.BlockSpec((pd, d), lambda i: (0, 0)),
    ]
    if final:
        args.append(g_final.reshape(1, d))
        specs.append(pl.BlockSpec((1, d), lambda i: (0, 0)))
    return pl.pallas_call(
        functools.partial(_ple_kernel, final=final),
        grid=(m // tm,),
        in_specs=specs,
        out_specs=pl.BlockSpec((tm, d), lambda i: (i, 0)),
        out_shape=jax.ShapeDtypeStruct((m, d), F32),
        compiler_params=_cparams("parallel"),
        name="ple",
    )(*args)


def _sb_qkv_seq_kernel(*refs, heads, aliased):
    x_ref, g_ref, wqt_ref, wkt_ref, wvt_ref, wk_ref = refs[:6]
    kt_ref, vt_ref, qt_ref, vtb_ref, kh_ref = refs[6 + (2 if aliased else 0):]
    xn = _rms(x_ref[...], g_ref[...]).astype(BF16)
    qt_ref[...] = _dot_nt(wqt_ref[...], xn).astype(BF16)
    kt_ref[...] = _dot_nt(wkt_ref[...], xn)
    vt = _dot_nt(wvt_ref[...], xn)
    vt_ref[...] = vt
    vtb_ref[...] = vt.astype(BF16)
    k = _dot(xn, wk_ref[...])
    for h in range(heads):
        kh_ref[h] = k[:, h * SB_HEAD_DIM:(h + 1) * SB_HEAD_DIM].astype(BF16)


def _sb_qkv_seq(r, g, w_qkv, layer, n_layers, kt_all, vt_all, seq_len):
    m, d = r.shape
    heads = d // SB_HEAD_DIM
    tm = 256
    tps = seq_len // tm
    wq, wk, wv = w_qkv[:, :d], w_qkv[:, d:2 * d], w_qkv[:, 2 * d:]
    const = lambda i: (0, 0)
    wspec = pl.BlockSpec((d, d), const)
    aliased = kt_all is not None
    args = [r, g.reshape(1, d), wq.T, wk.T, wv.T, wk]
    specs = [pl.BlockSpec((tm, d), lambda i: (i, 0)), pl.BlockSpec((1, d), const), wspec, wspec, wspec, wspec]
    aliases = {}
    if aliased:
        args += [kt_all, vt_all]
        specs += [pl.BlockSpec(memory_space=pl.ANY), pl.BlockSpec(memory_space=pl.ANY)]
        aliases = {6: 0, 7: 1}
    layer_spec = pl.BlockSpec((None, None, d, tm), lambda i: (layer, i // tps, 0, i % tps))
    layer_shape = jax.ShapeDtypeStruct((n_layers, m // seq_len, d, seq_len), F32)
    t_spec = pl.BlockSpec((d, tm), lambda i: (0, i))
    return pl.pallas_call(
        functools.partial(_sb_qkv_seq_kernel, heads=heads, aliased=aliased),
        grid=(m // tm,),
        in_specs=specs,
        out_specs=[layer_spec, layer_spec, t_spec, t_spec,
                   pl.BlockSpec((heads, tm, SB_HEAD_DIM), lambda i: (0, i, 0))],
        out_shape=[layer_shape, layer_shape,
                   jax.ShapeDtypeStruct((d, m), BF16), jax.ShapeDtypeStruct((d, m), BF16),
                   jax.ShapeDtypeStruct((heads, m, SB_HEAD_DIM), BF16)],
        input_output_aliases=aliases,
        compiler_params=_cparams("parallel"),
        name="sb_qkv_seq",
    )(*args)


def _sublane_suffix_exclusive_product(x):
    idx = lax.broadcasted_iota(jnp.int32, x.shape, 0)
    y = jnp.where(idx + 1 < SUBLANES, pltpu.roll(x, SUBLANES - 1, 0), 1.0)
    y = y * jnp.where(idx + 1 < SUBLANES, pltpu.roll(y, SUBLANES - 1, 0), 1.0)
    y = y * jnp.where(idx + 2 < SUBLANES, pltpu.roll(y, SUBLANES - 2, 0), 1.0)
    y = y * jnp.where(idx + 4 < SUBLANES, pltpu.roll(y, SUBLANES - 4, 0), 1.0)
    return y


def _sb_seq_kernel(q_ref, k_ref, v_ref, o_ref, kp_ref, vp_ref, z_ref, e_ref, a_ref, f_ref, acc_ref, c_ref,
                   *, nq, hp):
    tb = SB_BLOCK
    dh = SB_HEAD_DIM
    seg_idx = lax.broadcasted_iota(jnp.int32, (SUBLANES, tb), 0) * SB_SEG
    lane_idx = lax.broadcasted_iota(jnp.int32, (SUBLANES, tb), 1)
    slot = lax.broadcasted_iota(jnp.int32, (tb, tb), 0)
    key = lax.broadcasted_iota(jnp.int32, (tb, tb), 1)
    perm01 = jnp.where(key == (slot & (SUBLANES - 1)) * SB_SEG + (slot >> 3), 1.0, 0.0).astype(BF16)

    for kt in range(nq):
        cols = slice(kt * tb, (kt + 1) * tb)
        vp_ref[:, cols] = _dot_nt(v_ref[:, cols], perm01).astype(BF16)
        for hs in range(hp):
            kp_ref[hs, cols, :] = _dot(perm01, k_ref[hs, cols, :]).astype(BF16)


    def scores(qts, kj):
        k0 = pl.multiple_of(kj * tb, tb)
        return [_dot(kp_ref[hs, pl.ds(k0, tb), :], qts[hs]) for hs in range(hp)]

    def weighted_values(slot, kj):
        k0 = pl.multiple_of(kj * tb, tb)
        return [_dot(vp_ref[hs * dh:(hs + 1) * dh, pl.ds(k0, tb)], a_ref[slot, hs]) for hs in range(hp)]

    def accumulate(slot, parts):
        for hs in range(hp):
            acc_ref[hs * dh:(hs + 1) * dh, :] += parts[hs] * f_ref[slot, hs:hs + 1, :]

    def weights(slot, hs, diag):
        run = jnp.ones((SUBLANES, tb), F32)
        for r in range(SB_SEG - 1, -1, -1):
            rows = slice(r * SUBLANES, (r + 1) * SUBLANES)
            one_minus_beta = 1.0 / (1.0 + jnp.exp(z_ref[slot, hs, rows, :]))
            beta = 1.0 - one_minus_beta
            if diag:
                keep = seg_idx + r < lane_idx
                one_minus_beta = jnp.where(keep, one_minus_beta, 1.0)
                beta = jnp.where(keep, beta, 0.0)
            e_ref[hs, rows, :] = beta * run
            run = run * one_minus_beta
        later = _sublane_suffix_exclusive_product(run)
        for r in range(0, SB_SEG, 2):
            parts = [e_ref[hs, rr * SUBLANES:(rr + 1) * SUBLANES, :] * later for rr in (r, r + 1)]
            a_ref[slot, hs, r * SUBLANES:(r + 2) * SUBLANES, :] = jnp.concatenate(parts, axis=0).astype(BF16)
        return (later * run)[0:1, :]

    def q_body(qi, _):
        q0 = pl.multiple_of(qi * tb, tb)
        qts = [(q_ref[hs * dh:(hs + 1) * dh, pl.ds(q0, tb)].astype(F32) * (dh ** -0.5)).astype(BF16)
               for hs in range(hp)]
        z_first = scores(qts, qi)
        z_next = scores(qts, jnp.maximum(qi - 1, 0))
        acc_ref[...] = jnp.zeros_like(acc_ref)
        for hs in range(hp):
            z_ref[0, hs] = z_first[hs]
            z_ref[1, hs] = z_next[hs]
        for hs in range(hp):
            c_ref[hs:hs + 1, :] = weights(0, hs, True)
            f_ref[0, hs:hs + 1, :] = jnp.ones((1, tb), F32)

        def step(s, slot, prefetch=True):
            prev = 1 - slot
            parts = weighted_values(prev, qi - s + 1)
            if prefetch:
                z_next = scores(qts, jnp.maximum(qi - s - 1, 0))
                for hs in range(hp):
                    z_ref[prev, hs] = z_next[hs]
            for hs in range(hp):
                c_old = c_ref[hs:hs + 1, :]
                f_ref[slot, hs:hs + 1, :] = c_old
                c_ref[hs:hs + 1, :] = c_old * weights(slot, hs, False)
            accumulate(prev, parts)

        def pair_body(p, _):
            step(2 * p + 1, 1)
            step(2 * p + 2, 0)
            return 0

        lax.fori_loop(0, qi // 2, pair_body, 0)

        @pl.when(qi % 2 == 1)
        def _():
            step(qi, 1, prefetch=False)
            accumulate(1, weighted_values(1, 0))

        @pl.when(qi % 2 == 0)
        def _():
            accumulate(0, weighted_values(0, 0))

        o_ref[pl.ds(q0, tb), :] = jnp.transpose(acc_ref[...]).astype(o_ref.dtype)
        return 0

    lax.fori_loop(0, nq, q_body, 0)


def _sb_seq(q_t, k_h, v_t, batch, seq_len):
    d, m = q_t.shape
    heads = d // SB_HEAD_DIM
    hp = SB_HEADS_PER_STEP
    nq = seq_len // SB_BLOCK
    t_spec = pl.BlockSpec((hp * SB_HEAD_DIM, seq_len), lambda b, j: (j, b))
    return pl.pallas_call(
        functools.partial(_sb_seq_kernel, nq=nq, hp=hp),
        grid=(batch, heads // hp),
        in_specs=[t_spec, pl.BlockSpec((hp, seq_len, SB_HEAD_DIM), lambda b, j: (j, b, 0)), t_spec],
        out_specs=pl.BlockSpec((seq_len, hp * SB_HEAD_DIM), lambda b, j: (b, j)),
        out_shape=jax.ShapeDtypeStruct((m, d), BF16),
        scratch_shapes=[pltpu.VMEM((hp, seq_len, SB_HEAD_DIM), BF16),
                        pltpu.VMEM((hp * SB_HEAD_DIM, seq_len), BF16),
                        pltpu.VMEM((2, hp, SB_BLOCK, SB_BLOCK), F32),
                        pltpu.VMEM((hp, SB_BLOCK, SB_BLOCK), F32),
                        pltpu.VMEM((2, hp, SB_BLOCK, SB_BLOCK), BF16),
                        pltpu.VMEM((2, SUBLANES, SB_BLOCK), F32),
                        pltpu.VMEM((hp * SB_HEAD_DIM, SB_BLOCK), F32),
                        pltpu.VMEM((SUBLANES, SB_BLOCK), F32)],
        compiler_params=_cparams("parallel", "parallel"),
        name="sb_seq",
    )(q_t, k_h, v_t)


def _sb_step_kernel(pt_ref, q_ref, *refs, heads, pps):
    k_refs = refs[:pps]
    v_refs = refs[pps:2 * pps]
    o_ref, qb_ref, acc_ref, carry_ref = refs[2 * pps:]
    p = pl.program_id(1)
    d = heads * SB_HEAD_DIM
    own = (lax.broadcasted_iota(jnp.int32, (heads, d), 1) // SB_HEAD_DIM
           == lax.broadcasted_iota(jnp.int32, (heads, d), 0))

    @pl.when(p == 0)
    def _():
        q = jnp.broadcast_to(q_ref[...], (heads, d)) * (SB_HEAD_DIM ** -0.5)
        qb_ref[...] = jnp.where(own, q, 0.0).astype(BF16)
        acc_ref[...] = jnp.zeros_like(acc_ref)
        carry_ref[...] = jnp.zeros_like(carry_ref)

    row = lax.broadcasted_iota(jnp.int32, (PAGE, PAGE), 0)
    col = lax.broadcasted_iota(jnp.int32, (PAGE, PAGE), 1)
    suffix01 = jnp.where(row >= col, 1.0, 0.0).astype(BF16)
    qb = qb_ref[...]
    zs = [_dot(qb, k_refs[u][...].astype(BF16)) for u in range(pps)]
    splits = [_split3(_softplus(z)) for z in zs]
    cums = [_dot(hi, suffix01) + _dot(mid, suffix01) + _dot(lo, suffix01)
            for hi, mid, lo in splits]
    carries = [carry_ref[...]]
    for u in range(pps):
        carries.append(carries[u] + cums[u][:, 0:1])
    ws = [jnp.exp(zs[u] - cums[u] - carries[u]).astype(BF16) for u in range(pps)]
    parts = [_dot_nt(ws[u], v_refs[u][...].astype(BF16)) for u in range(pps)]
    acc = acc_ref[...]
    for part in parts:
        acc = acc + part
    acc_ref[...] = acc
    carry_ref[...] = carries[pps]

    @pl.when(p == pl.num_programs(1) - 1)
    def _():
        o_ref[...] = jnp.sum(jnp.where(own, acc, 0.0), axis=0, keepdims=True)


def _sb_step(q, page_table, cache_k, cache_v, layer):
    b, d = q.shape
    n_pages = page_table.shape[1]
    heads = d // SB_HEAD_DIM
    pps = _pick_tile(n_pages, (SB_PAGES_PER_STEP, 4, 2, 1))

    def page_spec(u):
        return pl.BlockSpec((None, None, d, PAGE),
                            lambda i, p, pt: (layer, pt[i, n_pages - 1 - (p * pps + u)], 0, 0))

    row_spec = pl.BlockSpec((None, 1, d), lambda i, p, pt: (i, 0, 0))
    page_specs = [page_spec(u) for u in range(pps)]
    grid_spec = pltpu.PrefetchScalarGridSpec(
        num_scalar_prefetch=1,
        grid=(b, n_pages // pps),
        in_specs=[row_spec] + page_specs + page_specs,
        out_specs=row_spec,
        scratch_shapes=[pltpu.VMEM((heads, d), BF16), pltpu.VMEM((heads, d), F32),
                        pltpu.VMEM((heads, 1), F32)],
    )
    out = pl.pallas_call(
        functools.partial(_sb_step_kernel, heads=heads, pps=pps),
        grid_spec=grid_spec,
        out_shape=jax.ShapeDtypeStruct((b, 1, d), F32),
        compiler_params=_cparams("parallel", "arbitrary"),
        name="sb_step",
    )(page_table, q.reshape(b, 1, d), *([cache_k] * pps), *([cache_v] * pps))
    return out.reshape(b, d)


def _gmlp_kernel(*refs, single, half, tm):
    if single:
        (x_ref, g_ref, win_ref, bin_ref, lng_ref, lnb_ref, wsv_ref, bsv_ref, wo_ref,
         o_ref, vout_ref, vn_ref, acc_ref) = refs
    else:
        (x_ref, g_ref, win_ref, bin_ref, lng_ref, lnb_ref, ws_ref, bst_ref, wo_ref,
         o_ref, vn_ref, acc_ref) = refs
    gd = half // CM_GROUPS
    pair = 2 * gd
    x = x_ref[...]
    xn = _rms(x, g_ref[...]).astype(BF16)

    v = _gelu(_dot(xn, win_ref[:, half:]) + bin_ref[:, half:])
    mu = jnp.mean(v, axis=-1, keepdims=True)
    var = jnp.mean(jnp.square(v - mu), axis=-1, keepdims=True)
    vn = (v - mu) * lax.rsqrt(var + EPS) * lng_ref[...] + lnb_ref[...]
    if single:
        vout_ref[...] = vn
        vn_ref[...] = vn
    else:
        vn_ref[...] = vn.astype(BF16)
        row = lax.broadcasted_iota(jnp.int32, (CM_CHUNK, CM_CHUNK), 0)
        col = lax.broadcasted_iota(jnp.int32, (CM_CHUNK, CM_CHUNK), 1)
        causal = col <= row

    for gp in range(CM_GROUPS // 2):
        lo = gp * pair
        u = _gelu(_dot(xn, win_ref[:, lo:lo + pair]) + bin_ref[:, lo:lo + pair])
        if single:
            s = vn_ref[:, lo:lo + pair] * wsv_ref[:, lo:lo + pair] + bsv_ref[:, lo:lo + pair]
        else:
            cols = []
            for gi in range(2):
                g = 2 * gp + gi
                ws = jnp.where(causal, ws_ref[g], 0.0).astype(BF16)
                bs = bst_ref[:, g:g + 1]
                rows = []
                for c in range(tm // CM_CHUNK):
                    vc = vn_ref[c * CM_CHUNK:(c + 1) * CM_CHUNK, g * gd:(g + 1) * gd]
                    rows.append(_dot(ws, vc) + bs)
                cols.append(jnp.concatenate(rows, axis=0) if len(rows) > 1 else rows[0])
            s = jnp.concatenate(cols, axis=1)
        y = (u * s).astype(BF16)
        contrib = _dot(y, wo_ref[lo:lo + pair, :])
        if gp == 0:
            acc_ref[...] = contrib
        else:
            acc_ref[...] += contrib
    o_ref[...] = x + acc_ref[...]


def _gmlp(r, g, w_in, b_in, ln_g, ln_b, w_s, b_s, w_o, single):
    m, d = r.shape
    half = w_o.shape[0]
    gd = half // CM_GROUPS
    tm = min(m, 128) if single else 256
    const = lambda i: (0, 0)
    specs = [
        pl.BlockSpec((tm, d), lambda i: (i, 0)),
        pl.BlockSpec((1, d), const),
        pl.BlockSpec((d, 2 * half), const),
        pl.BlockSpec((1, 2 * half), const),
        pl.BlockSpec((1, half), const),
        pl.BlockSpec((1, half), const),
    ]
    args = [r, g.reshape(1, d), w_in, b_in.reshape(1, -1), ln_g.reshape(1, -1), ln_b.reshape(1, -1)]
    if single:
        args += [jnp.repeat(w_s[:, 0, 0], gd).reshape(1, half), jnp.repeat(b_s[:, 0], gd).reshape(1, half)]
        specs += [pl.BlockSpec((1, half), const), pl.BlockSpec((1, half), const)]
    else:
        args += [w_s, jnp.transpose(b_s)]
        specs += [pl.BlockSpec((CM_GROUPS, CM_CHUNK, CM_CHUNK), lambda i: (0, 0, 0)),
                  pl.BlockSpec((CM_CHUNK, CM_GROUPS), const)]
    args.append(w_o)
    specs.append(pl.BlockSpec((half, d), const))
    row_out = pl.BlockSpec((tm, d), lambda i: (i, 0))
    if single:
        out_specs = [row_out, pl.BlockSpec((tm, half), lambda i: (i, 0))]
        out_shape = [jax.ShapeDtypeStruct((m, d), F32), jax.ShapeDtypeStruct((m, half), F32)]
        scratch = [pltpu.VMEM((tm, half), F32), pltpu.VMEM((tm, d), F32)]
    else:
        out_specs = row_out
        out_shape = jax.ShapeDtypeStruct((m, d), F32)
        scratch = [pltpu.VMEM((tm, half), BF16), pltpu.VMEM((tm, d), F32)]
    return pl.pallas_call(
        functools.partial(_gmlp_kernel, single=single, half=half, tm=tm),
        grid=(m // tm,),
        in_specs=specs,
        out_specs=out_specs,
        out_shape=out_shape,
        scratch_shapes=scratch,
        compiler_params=_cparams("parallel"),
        name="gmlp_step" if single else "gmlp_seq",
    )(*args)


def _ml_qkv_tail(xc, x, wq_ref, wk_ref, wv_ref, q_ref, k_ref, v_ref, scale):
    xc = xc.astype(BF16)
    q_ref[...] = _dot(xc, wq_ref[...]).astype(q_ref.dtype)
    k_ref[...] = (_dot(xc, wk_ref[...]) * scale).astype(k_ref.dtype)
    v_ref[...] = _dot(x.astype(BF16), wv_ref[...]).astype(v_ref.dtype)


def _ml_qkv_seq_kernel(x_ref, halo_ref, cw_ref, cb_ref, wq_ref, wk_ref, wv_ref, q_ref, k_ref, v_ref, xs_ref,
                       *, tiles_per_seq, tm, scale):
    i = pl.program_id(1)
    x = x_ref[...]
    xs_ref[:ML_HALO, :] = jnp.where(i % tiles_per_seq == 0, 0.0, halo_ref[...])
    xs_ref[ML_HALO:, :] = x
    y = cb_ref[...]
    for t in range(ML_CONV - 1):
        off = ML_HALO - (ML_CONV - 1) + t
        y = y + cw_ref[t:t + 1, :] * xs_ref[off:off + tm, :]
    y = y + cw_ref[ML_CONV - 1:ML_CONV, :] * x
    _ml_qkv_tail(_silu(y), x, wq_ref, wk_ref, wv_ref, q_ref, k_ref, v_ref, scale)


def _ml_qkv_step_kernel(x_ref, b0_ref, b1_ref, b2_ref, cw_ref, cb_ref, wq_ref, wk_ref, wv_ref,
                        q_ref, k_ref, v_ref, *, scale):
    x = x_ref[...]
    y = (cb_ref[...] + cw_ref[0:1, :] * b0_ref[...] + cw_ref[1:2, :] * b1_ref[...]
         + cw_ref[2:3, :] * b2_ref[...] + cw_ref[3:4, :] * x)
    _ml_qkv_tail(_silu(y), x, wq_ref, wk_ref, wv_ref, q_ref, k_ref, v_ref, scale)


def _ml_qkv(proj, conv_w, conv_b, w_q, w_k, w_v, *, seq_len=None, bufs=None, qv_dtype=BF16):
    m = proj.shape[0]
    nh, hd, _ = w_q.shape
    inner = nh * hd
    scale = hd ** -0.5
    cb = conv_b.reshape(1, inner)
    wspec = pl.BlockSpec((None, hd, hd), lambda h, i: (h, 0, 0))
    cwspec = pl.BlockSpec((ML_CONV, hd), lambda h, i: (0, h))
    cbspec = pl.BlockSpec((1, hd), lambda h, i: (0, h))
    if bufs is None:
        tm = _pick_tile(seq_len, (512, 256, 128))
        hb = tm // ML_HALO
        xspec = pl.BlockSpec((tm, hd), lambda h, i: (i, h))
        kern = functools.partial(_ml_qkv_seq_kernel, tiles_per_seq=seq_len // tm, tm=tm, scale=scale)
        args = [proj, proj, conv_w, cb, w_q, w_k, w_v]
        specs = [xspec, pl.BlockSpec((ML_HALO, hd), lambda h, i: (jnp.maximum(i * hb - 1, 0), h)),
                 cwspec, cbspec, wspec, wspec, wspec]
        scratch = [pltpu.VMEM((tm + ML_HALO, hd), F32)]
    else:
        tm = m
        xspec = pl.BlockSpec((tm, hd), lambda h, i: (i, h))
        kern = functools.partial(_ml_qkv_step_kernel, scale=scale)
        args = [proj, *bufs, conv_w, cb, w_q, w_k, w_v]
        specs = [xspec, xspec, xspec, xspec, cwspec, cbspec, wspec, wspec, wspec]
        scratch = []
    return pl.pallas_call(
        kern,
        grid=(nh, m // tm),
        in_specs=specs,
        out_specs=[xspec, xspec, xspec],
        out_shape=[jax.ShapeDtypeStruct((m, inner), qv_dtype), jax.ShapeDtypeStruct((m, inner), F32),
                   jax.ShapeDtypeStruct((m, inner), qv_dtype)],
        scratch_shapes=scratch,
        compiler_params=_cparams("parallel", "parallel"),
        name="ml_qkv",
    )(*args)


def _ml_in_kernel(x_ref, g_ref, w_ref, wg_ref, bg_ref, o_ref, og_ref):
    xn = _rms(x_ref[...], g_ref[...]).astype(BF16)
    o_ref[...] = _dot(xn, w_ref[...])
    og_ref[...] = _dot(xn, wg_ref[...]) + bg_ref[...]


def _ml_in_seq(x, g, w_main, w_gate, b_gate):
    m, k = x.shape
    n = w_main.shape[1]
    tm = 256
    const = lambda i: (0, 0)
    return pl.pallas_call(
        _ml_in_kernel,
        grid=(m // tm,),
        in_specs=[pl.BlockSpec((tm, k), lambda i: (i, 0)), pl.BlockSpec((1, k), const),
                  pl.BlockSpec((k, n), const), pl.BlockSpec((k, LANES), const), pl.BlockSpec((1, LANES), const)],
        out_specs=[pl.BlockSpec((tm, n), lambda i: (i, 0)), pl.BlockSpec((tm, LANES), lambda i: (i, 0))],
        out_shape=[jax.ShapeDtypeStruct((m, n), F32), jax.ShapeDtypeStruct((m, LANES), F32)],
        compiler_params=_cparams("parallel"),
        name="ml_in",
    )(x, g.reshape(1, k), w_main, w_gate, b_gate.reshape(1, LANES))


def _lane_select(x, idx):
    lane = lax.broadcasted_iota(jnp.int32, x.shape, 1)
    return jnp.sum(jnp.where(lane == idx, x, 0.0), axis=1, keepdims=True)


def _ml_chunk_kernel(q_ref, k_ref, v_ref, gt_ref, h_ref, c_out, n_out, m_out, c_ref, n_ref, m_ref, *, heads):
    c = pl.program_id(1)
    L = ML_CHUNK
    hd = q_ref.shape[1] // heads
    hs = range(heads)
    cols = [slice(j * hd, (j + 1) * hd) for j in hs]

    @pl.when(c == 0)
    def _():
        c_ref[...] = jnp.zeros_like(c_ref)
        n_ref[...] = jnp.zeros_like(n_ref)
        m_ref[...] = jnp.zeros_like(m_ref)

    gt = gt_ref[...]
    row = lax.broadcasted_iota(jnp.int32, (L, L), 0)
    col = lax.broadcasted_iota(jnp.int32, (L, L), 1)
    causal = col <= row
    lower01 = jnp.where(causal, 1.0, 0.0).astype(BF16)
    ones01 = jnp.ones((L, L), BF16)
    i_col = [_lane_select(gt, j) for j in hs]
    f_col = [_log_sigmoid(_lane_select(gt, j + heads)) for j in hs]
    f_b = [jnp.broadcast_to(f_col[j], (L, L)) for j in hs]
    i_b = [jnp.broadcast_to(i_col[j], (L, L)) for j in hs]
    parts_c = [_split3(f_b[j]) for j in hs]
    parts_r = [_split3(jnp.where(row <= col, f_b[j], 0.0)) for j in hs]
    parts_i = [_split3(jnp.where(row == col, i_b[j], 0.0)) for j in hs]
    dot3 = lambda lhs, p: _dot(lhs, p[0]) + _dot(lhs, p[1]) + _dot(lhs, p[2])
    b_c = [dot3(lower01, parts_c[j]) for j in hs]
    b_r = [dot3(ones01, parts_r[j]) for j in hs]
    i_r = [dot3(ones01, parts_i[j]) for j in hs]

    b_col = [b_c[j][:, 0:1] for j in hs]
    m_prev = [m_ref[j, 0:1, 0:1] for j in hs]
    d_log = [jnp.where(causal, b_c[j] - b_r[j] + i_r[j], -jnp.inf) for j in hs]
    inter = [b_col[j] + m_prev[j] for j in hs]
    m_t = [jnp.maximum(inter[j], jnp.max(d_log[j], axis=1, keepdims=True)) for j in hs]
    w_intra = [jnp.exp(d_log[j] - m_t[j]) for j in hs]
    w_inter = [jnp.exp(inter[j] - m_t[j]) for j in hs]

    q = [q_ref[:, cols[j]].astype(BF16) for j in hs]
    k = [k_ref[:, cols[j]] for j in hs]
    kb = [k[j].astype(BF16) for j in hs]
    v = [v_ref[:, cols[j]].astype(BF16) for j in hs]
    c_old = [c_ref[j] for j in hs]
    n_old = [n_ref[j] for j in hs]
    s = [_dot_nt(q[j], kb[j]) for j in hs]
    q_c = [_dot(q[j], c_old[j].astype(BF16)) for j in hs]
    q_n = [_dot_nt(q[j], n_old[j].astype(BF16))[:, 0:1] for j in hs]
    qk = [s[j] * w_intra[j] for j in hs]
    qk_v = [_dot(qk[j].astype(BF16), v[j]) for j in hs]
    for j in hs:
        num = w_inter[j] * q_c[j] + qk_v[j]
        den = w_inter[j] * q_n[j] + jnp.sum(qk[j], axis=1, keepdims=True)
        h_ref[:, cols[j]] = num / jnp.maximum(jnp.abs(den), jnp.exp(-m_t[j]))

    m_new = [m_t[j][L - 1:L, :] for j in hs]
    b_last = [b_col[j][L - 1:L, :] for j in hs]
    w_state = [jnp.exp(b_last[j] + m_prev[j] - m_new[j]) for j in hs]
    w_rows_c = [jnp.exp(b_last[j] - b_col[j] + i_col[j] - m_new[j]) for j in hs]
    w_rows_r = [jnp.exp(b_last[j] - b_r[j][0:SUBLANES, :] + i_r[j][0:SUBLANES, :] - m_new[j])
                for j in hs]
    kw = [(k[j] * w_rows_c[j]).astype(BF16) for j in hs]
    c_upd = [_dot_tn(kw[j], v[j]) for j in hs]
    n_upd = [_dot(w_rows_r[j].astype(BF16), kb[j]) for j in hs]
    for j in hs:
        c_ref[j] = w_state[j] * c_old[j] + c_upd[j]
        n_ref[j] = w_state[j] * n_old[j] + n_upd[j]
        m_ref[j] = jnp.broadcast_to(m_new[j], m_ref.shape[1:])

    @pl.when(c == pl.num_programs(1) - 1)
    def _():
        c_out[...] = c_ref[...]
        for j in hs:
            n_out[j] = n_ref[j, 0:1, :]
            m_out[j] = m_ref[j, 0:1, :]


def _ml_chunks(q, k, v, gates, batch, seq_len, heads):
    m, inner = q.shape
    hd = inner // heads
    nc = seq_len // ML_CHUNK
    xspec = pl.BlockSpec((ML_CHUNK, inner), lambda b, c: (b * nc + c, 0))
    return pl.pallas_call(
        functools.partial(_ml_chunk_kernel, heads=heads),
        grid=(batch, nc),
        in_specs=[xspec, xspec, xspec, pl.BlockSpec((ML_CHUNK, LANES), lambda b, c: (b * nc + c, 0))],
        out_specs=[xspec,
                   pl.BlockSpec((None, heads, hd, hd), lambda b, c: (b, 0, 0, 0)),
                   pl.BlockSpec((None, heads, 1, hd), lambda b, c: (b, 0, 0, 0)),
                   pl.BlockSpec((None, heads, 1, LANES), lambda b, c: (b, 0, 0, 0))],
        out_shape=[jax.ShapeDtypeStruct((m, inner), F32),
                   jax.ShapeDtypeStruct((batch, heads, hd, hd), F32),
                   jax.ShapeDtypeStruct((batch, heads, 1, hd), F32),
                   jax.ShapeDtypeStruct((batch, heads, 1, LANES), F32)],
        scratch_shapes=[pltpu.VMEM((heads, hd, hd), F32), pltpu.VMEM((heads, SUBLANES, hd), F32),
                        pltpu.VMEM((heads, SUBLANES, LANES), F32)],
        compiler_params=_cparams("parallel", "arbitrary"),
        name="ml_chunks",
    )(q, k, v, gates)


def _ml_step_kernel(q_ref, k_ref, v_ref, gt_ref, mp_ref, c_ref, n_ref, h_ref, c_out, n_out, m_out, *, heads):
    gt = gt_ref[...]
    mp = mp_ref[...]
    hd = q_ref.shape[1] // heads
    lane = lax.broadcasted_iota(jnp.int32, (1, LANES), 1)
    first_row = lax.broadcasted_iota(jnp.int32, (SUBLANES, hd), 0) == 0
    m_all = jnp.zeros((1, LANES), F32)
    for j in range(heads):
        cols = slice(j * hd, (j + 1) * hd)
        i_g = _lane_select(gt, j)
        f_g = _log_sigmoid(_lane_select(gt, j + heads))
        inter = f_g + _lane_select(mp, j)
        m_t = jnp.maximum(inter, i_g)
        w_in = jnp.exp(i_g - m_t)
        w_st = jnp.exp(inter - m_t)

        q = q_ref[:, cols]
        k = k_ref[:, cols]
        v = v_ref[:, cols]
        n = n_ref[:, cols]
        c_old = c_ref[j]
        q8 = jnp.broadcast_to(q, (SUBLANES, hd)).astype(BF16)
        q_c = _dot(q8, c_old.astype(BF16))[0:1, :]
        qk = jnp.sum(q * k, axis=1, keepdims=True) * w_in
        num = w_st * q_c + qk * v
        den = w_st * jnp.sum(q * n, axis=1, keepdims=True) + qk
        h_ref[:, cols] = num / jnp.maximum(jnp.abs(den), jnp.exp(-m_t))

        kw = k * w_in
        kw8 = jnp.where(first_row, jnp.broadcast_to(kw, (SUBLANES, hd)), 0.0).astype(BF16)
        v8 = jnp.broadcast_to(v, (SUBLANES, hd)).astype(BF16)
        c_out[j] = w_st * c_old + _dot_tn(kw8, v8)
        n_out[:, cols] = w_st * n + kw
        m_all = jnp.where(lane == j, m_t, m_all)
    m_out[...] = m_all


def _ml_step(q, k, v, gates, m_prev, c_state, n_state, heads):
    b, inner = q.shape
    hd = inner // heads
    r3 = lambda a: a.reshape(b, 1, -1)
    vspec = pl.BlockSpec((None, 1, inner), lambda i: (i, 0, 0))
    gspec = pl.BlockSpec((None, 1, LANES), lambda i: (i, 0, 0))
    cspec = pl.BlockSpec((None, heads, hd, hd), lambda i: (i, 0, 0, 0))
    return pl.pallas_call(
        functools.partial(_ml_step_kernel, heads=heads),
        grid=(b,),
        in_specs=[vspec, vspec, vspec, gspec, gspec, cspec, vspec],
        out_specs=[vspec, cspec, vspec, gspec],
        out_shape=[jax.ShapeDtypeStruct((b, 1, inner), F32),
                   jax.ShapeDtypeStruct((b, heads, hd, hd), F32),
                   jax.ShapeDtypeStruct((b, 1, inner), F32),
                   jax.ShapeDtypeStruct((b, 1, LANES), F32)],
        compiler_params=_cparams("parallel"),
        name="ml_step",
    )(r3(q), r3(k), r3(v), r3(gates), r3(m_prev), c_state, r3(n_state))


def _ml_out_kernel(h_ref, o_ref, ng_ref, wo_ref, r_ref, out_ref, *, heads):
    h = h_ref[...]
    hd = h.shape[1] // heads
    parts = []
    for j in range(heads):
        hh = h[:, j * hd:(j + 1) * hd]
        mu = jnp.mean(hh, axis=-1, keepdims=True)
        var = jnp.mean(jnp.square(hh - mu), axis=-1, keepdims=True)
        parts.append((hh - mu) * lax.rsqrt(var + EPS))
    hn = jnp.concatenate(parts, axis=1) * ng_ref[...]
    out = (_sigmoid(o_ref[...]) * hn).astype(BF16)
    out_ref[...] = r_ref[...] + _dot(out, wo_ref[...])


def _ml_out(h, proj, norm_g, w_o, r, heads):
    m, inner = h.shape
    d = r.shape[1]
    tm = _pick_tile(m, (256, 128))
    return pl.pallas_call(
        functools.partial(_ml_out_kernel, heads=heads),
        grid=(m // tm,),
        in_specs=[pl.BlockSpec((tm, inner), lambda i: (i, 0)),
                  pl.BlockSpec((tm, inner), lambda i: (i, 1)),
                  pl.BlockSpec((1, inner), lambda i: (0, 0)),
                  pl.BlockSpec((inner, d), lambda i: (0, 0)),
                  pl.BlockSpec((tm, d), lambda i: (i, 0))],
        out_specs=pl.BlockSpec((tm, d), lambda i: (i, 0)),
        out_shape=jax.ShapeDtypeStruct((m, d), F32),
        compiler_params=_cparams("parallel"),
        name="ml_out",
    )(h, proj, norm_g.reshape(1, inner), w_o, r)


def _trunk(x, p, seq_len, weights, *, past=None, ml_state=None, ffn_state=None):
    (norm_mix, norm_ffn, norm_ple, norm_final, sb_w_qkv, sb_w_o, cm_w_in, cm_b_in, cm_ln_g, cm_ln_b,
     cm_w_s, cm_b_s, cm_w_o, ml_w_in, ml_b_gates, ml_conv_w, ml_conv_b, ml_w_q, ml_w_k, ml_w_v,
     ml_norm_g, ml_w_o, ffn_w_up, ffn_conv_w, ffn_conv_b, ffn_w_down, ple_w_proj, ple_w_gate) = weights
    m, d = x.shape
    batch = m // seq_len
    depth = norm_mix.shape[0]
    n_sb = sb_w_qkv.shape[0]
    step = seq_len == 1
    heads = d // SB_HEAD_DIM
    outs = dict(k=[], v=[], cm=[], c=[], n=[], m=[], conv=[], ffn=[])
    kt_all = vt_all = None
    r = x
    for i in range(depth):
        kind, j = i % 3, i // 3
        if kind == 0:
            if step:
                qkv = _matmul(r, sb_w_qkv[j], g=norm_mix[i], name="sb_qkv")
                q, k, v = qkv[:, :d], qkv[:, d:2 * d], qkv[:, 2 * d:]
                outs["k"].append(k.reshape(batch, seq_len, heads, SB_HEAD_DIM))
                outs["v"].append(v.reshape(batch, seq_len, heads, SB_HEAD_DIM))
                page_table, cache_k, cache_v = past
                o = _sb_step(q, page_table, cache_k, cache_v, j)
            else:
                kt_all, vt_all, q_t, v_t, k_h = _sb_qkv_seq(r, norm_mix[i], sb_w_qkv[j], j, n_sb, kt_all, vt_all,
                                                           seq_len)
                o = _sb_seq(q_t, k_h, v_t, batch, seq_len)
            if step:
                r = _matmul(o, sb_w_o[j], res=r, name="sb_out")
            else:
                r = _matmul_rows(o, sb_w_o[j], tm=512, res=r, name="sb_out")
        elif kind == 1:
            res = _gmlp(r, norm_mix[i], cm_w_in[j], cm_b_in[j], cm_ln_g[j], cm_ln_b[j], cm_w_s[j], cm_b_s[j],
                        cm_w_o[j], step)
            if step:
                r, vn = res
                outs["cm"].append(vn.reshape(batch, 1, -1))
            else:
                r = res
        else:
            nh = ML_HEADS
            inner = ml_w_q.shape[2] * nh
            w_main = ml_w_in[j][:, :2 * inner]
            w_gate = jnp.pad(ml_w_in[j][:, 2 * inner:], ((0, 0), (0, LANES - 2 * nh)))
            b_gate = jnp.pad(ml_b_gates[j], (0, LANES - 2 * nh))
            if step:
                proj = _matmul(r, w_main, g=norm_mix[i], name="ml_in")
                gates = _matmul(r, w_gate, g=norm_mix[i], bias=b_gate, name="ml_gates")
            else:
                proj, gates = _ml_in_seq(r, norm_mix[i], w_main, w_gate, b_gate)
            if step:
                c0, n0, m0, conv0 = ml_state
                bufs = [conv0[j][:, t] for t in range(ML_CONV - 1)]
                q, k, v = _ml_qkv(proj, ml_conv_w[j], ml_conv_b[j], ml_w_q[j], ml_w_k[j], ml_w_v[j],
                                  bufs=bufs, qv_dtype=F32)
                m_prev = jnp.pad(m0[j], ((0, 0), (0, LANES - nh)))
                hh, c_new, n_new, m_new = _ml_step(q, k, v, gates, m_prev, c0[j], n0[j], nh)
                hh = hh.reshape(m, inner)
                m_new = m_new[:, 0, :nh]
                outs["conv"].append(jnp.concatenate([conv0[j][:, 1:], proj[:, None, :inner]], axis=1))
            else:
                q, k, v = _ml_qkv(proj, ml_conv_w[j], ml_conv_b[j], ml_w_q[j], ml_w_k[j], ml_w_v[j],
                                  seq_len=seq_len)
                hh, c_new, n_new, m_new = _ml_chunks(q, k, v, gates, batch, seq_len, nh)
                m_new = m_new[:, :, 0, 0]
                tail = proj.reshape(batch, seq_len, -1)[:, seq_len - (ML_CONV - 1):, :inner]
                outs["conv"].append(tail)
            outs["c"].append(c_new)
            outs["n"].append(n_new.reshape(batch, nh, -1))
            outs["m"].append(m_new)
            r = _ml_out(hh, proj, ml_norm_g[j], ml_w_o[j], r, nh)

        g_final = norm_final if i == depth - 1 else None
        if step:
            a = _matmul(r, ffn_w_up[i], g=norm_ffn[i], name="ffn_up")
            buf = ffn_state[i]
            r = _ffn_step(r, a, buf[:, 0], buf[:, 1], ffn_conv_w[i], ffn_conv_b[i], ffn_w_down[i])
            outs["ffn"].append(jnp.concatenate([buf[:, 1:], a[:, None, :]], axis=1))
            r = _ple(r, norm_ple[i], ple_w_gate[i], p[i], ple_w_proj[i], g_final=g_final)
        else:
            tail = r.reshape(batch, seq_len, d)[:, seq_len - (FFN_CONV - 1):].reshape(-1, d)
            a_tail = _matmul(tail, ffn_w_up[i], g=norm_ffn[i], name="ffn_tail")
            outs["ffn"].append(a_tail.reshape(batch, FFN_CONV - 1, -1))
            r = _ffn_ple_seq(r, norm_ffn[i], ffn_w_up[i], ffn_conv_w[i], ffn_conv_b[i], ffn_w_down[i], seq_len,
                             norm_ple[i], ple_w_gate[i], p[i], ple_w_proj[i], g_final=g_final)
    if not step:
        rows = lambda a: a.reshape(n_sb, batch, heads, SB_HEAD_DIM, seq_len).transpose(0, 1, 4, 2, 3)
        outs["k"], outs["v"] = rows(kt_all), rows(vt_all)
    else:
        outs["k"], outs["v"] = jnp.stack(outs["k"]), jnp.stack(outs["v"])
    return r, outs


def kernel(x_prompt, x_sample, p_prompt, p_sample, page_table, cache_k, cache_v, state_mlstm_c, state_mlstm_n, state_mlstm_m, state_mlstm_conv, state_ffn_conv, norm_mix, norm_ffn, norm_ple, norm_final, sb_w_qkv, sb_w_o, cm_w_in, cm_b_in, cm_ln_g, cm_ln_b, cm_w_s, cm_b_s, cm_w_o, ml_w_in, ml_b_gates, ml_conv_w, ml_conv_b, ml_w_q, ml_w_k, ml_w_v, ml_norm_g, ml_w_o, ffn_w_up, ffn_conv_w, ffn_conv_b, ffn_w_down, ple_w_proj, ple_w_gate):
    bf = lambda w: w.astype(BF16)
    weights = (norm_mix, norm_ffn, norm_ple, norm_final, bf(sb_w_qkv), bf(sb_w_o), bf(cm_w_in), cm_b_in,
               cm_ln_g, cm_ln_b, cm_w_s, cm_b_s, bf(cm_w_o), bf(ml_w_in), ml_b_gates, ml_conv_w, ml_conv_b,
               bf(ml_w_q), bf(ml_w_k), bf(ml_w_v), ml_norm_g, bf(ml_w_o), bf(ffn_w_up), ffn_conv_w,
               ffn_conv_b, bf(ffn_w_down), bf(ple_w_proj), bf(ple_w_gate))
    b, t, d = x_prompt.shape
    bs, ts, _ = x_sample.shape
    depth = p_prompt.shape[0]
    n_sb, n_phys, page, sbh, sbd = cache_k.shape
    pages = lambda c: c.transpose(0, 1, 3, 4, 2).reshape(n_sb, n_phys, sbh * sbd, page)

    y_p, o_p = _trunk(x_prompt.reshape(b * t, d), p_prompt.reshape(depth, b * t, -1), t, weights)
    y_s, o_s = _trunk(
        x_sample.reshape(bs * ts, d), p_sample.reshape(depth, bs * ts, -1), ts, weights,
        past=(page_table, pages(cache_k), pages(cache_v)),
        ml_state=(state_mlstm_c, state_mlstm_n, state_mlstm_m, state_mlstm_conv),
        ffn_state=state_ffn_conv)

    st = jnp.stack
    return (y_p.reshape(b, t, d), y_s.reshape(bs, ts, d),
            o_p["k"], o_p["v"], o_s["k"], o_s["v"], st(o_s["cm"]),
            st(o_p["c"]), st(o_p["n"]), st(o_p["m"]), st(o_p["conv"]),
            st(o_s["c"]), st(o_s["n"]), st(o_s["m"]), st(o_s["conv"]),
            st(o_p["ffn"]), st(o_s["ffn"]))
```
